```python
import jax
import jax.numpy as jnp
from jax import lax
import numpy as np

D_MODEL = 1024
BATCH = 16
SEQ = 2048
DEPTH = 1

ROPE_THETA = 500000.0
NSA_HEADS = 8
NSA_KV_GROUPS = 2
NSA_GROUP = NSA_HEADS // NSA_KV_GROUPS
NSA_HEAD_DIM = 64
NSA_ROT_DIM = NSA_HEAD_DIM // 4
CMP_BLOCK = 32
CMP_STRIDE = 16
CMP_HIDDEN = 256
SEL_BLOCK = 64
SEL_TOPK = 8
WINDOW = 256
MLA_HEADS = 8
MLA_Q_RANK = 768
MLA_KV_RANK = 256
MLA_NOPE_DIM = 64
MLA_ROPE_DIM = 32
MLA_V_DIM = 64
MLA_QK_DIM = MLA_NOPE_DIM + MLA_ROPE_DIM
D_FF = 2816
CONV_WIDTH = 3
Q_BLOCK = 128
SEL_Q_BLOCK = 64
LN_EPS = 1e-5
RMS_EPS = 1e-6
DEEPNORM_ALPHA = (2 * DEPTH) ** 0.25
DEEPNORM_BETA = (8 * DEPTH) ** -0.25

NSA_Q_WIDTH = NSA_HEADS * NSA_HEAD_DIM
NSA_KV_WIDTH = NSA_KV_GROUPS * NSA_HEAD_DIM
IN_WIDTHS = (NSA_Q_WIDTH, NSA_KV_WIDTH, NSA_KV_WIDTH, NSA_KV_WIDTH, NSA_KV_WIDTH, NSA_KV_WIDTH, NSA_KV_WIDTH,
             3 * NSA_HEADS, MLA_Q_RANK, MLA_KV_RANK, MLA_ROPE_DIM, 2 * D_MODEL)
IN_TOTAL = sum(IN_WIDTHS)
IN_OFFSETS = tuple(int(v) for v in np.cumsum(IN_WIDTHS)[:-1])

kernel_name = 'hybrid_nsa_mla_convglu_deepnorm'


def layer_norm(x, g, b):
    xf = x.astype(jnp.float32)
    mu = jnp.mean(xf, axis=-1, keepdims=True)
    var = jnp.mean(jnp.square(xf - mu), axis=-1, keepdims=True)
    return ((xf - mu) * lax.rsqrt(var + LN_EPS) * g + b).astype(x.dtype)


def rms_norm(x, g):
    xf = x.astype(jnp.float32)
    return (xf * lax.rsqrt(jnp.mean(jnp.square(xf), axis=-1, keepdims=True) + RMS_EPS) * g).astype(x.dtype)


def rope_tables(rot_dim, seq):
    pos = jnp.arange(seq, dtype=jnp.float32)
    inv = ROPE_THETA ** (-jnp.arange(0, rot_dim, 2, dtype=jnp.float32) / rot_dim)
    ang = pos[:, None] * inv[None, :]
    return jnp.cos(ang), jnp.sin(ang)


def apply_partial_rope(x, cos, sin, rot_dim):
    half = rot_dim // 2
    c = cos[:, None, :].astype(x.dtype)
    s = sin[:, None, :].astype(x.dtype)
    x1, x2, xp = x[..., :half], x[..., half:rot_dim], x[..., rot_dim:]
    return jnp.concatenate([x1 * c - x2 * s, x2 * c + x1 * s, xp], axis=-1)


def masked_softmax(scores, mask):
    s = jnp.where(mask, scores.astype(jnp.float32), -jnp.inf)
    m = jnp.max(s, axis=-1, keepdims=True)
    m = jnp.where(jnp.isfinite(m), m, 0.0)
    e = jnp.where(mask, jnp.exp(s - m), 0.0)
    return e / jnp.maximum(jnp.sum(e, axis=-1, keepdims=True), 1e-30)


def compress_tokens(tok, pe, w1, b1, w2):
    b, s, g, d = tok.shape
    n_cmp = (s - CMP_BLOCK) // CMP_STRIDE + 1
    idx = jnp.arange(n_cmp)[:, None] * CMP_STRIDE + jnp.arange(CMP_BLOCK)[None, :]
    blk = tok[:, idx] + pe[None, None, :, None, :]
    blk = jnp.swapaxes(blk, 2, 3).reshape(b, n_cmp, g, CMP_BLOCK * d)
    return jax.nn.gelu(blk @ w1 + b1) @ w2


def nsa_attention(q, k_cmp, v_cmp, k_sel, v_sel, k_win, v_win, gates,
                  pe_k, pe_v, ck_w1, ck_b1, ck_w2, cv_w1, cv_b1, cv_w2):
    b, s = q.shape[:2]
    scale = NSA_HEAD_DIM ** -0.5
    pos = jnp.arange(s)

    kc = compress_tokens(k_cmp, pe_k, ck_w1, ck_b1, ck_w2)
    vc = compress_tokens(v_cmp, pe_v, cv_w1, cv_b1, cv_w2)
    n_cmp = kc.shape[1]
    cmp_start = jnp.arange(n_cmp) * CMP_STRIDE
    cmp_end = cmp_start + CMP_BLOCK - 1
    sc = jnp.einsum('bshgd,bnhd->bhgsn', q, kc) * scale
    p_cmp = masked_softmax(sc, cmp_end[None, :] <= pos[:, None])
    o_cmp = jnp.einsum('bhgsn,bnhd->bshgd', p_cmp.astype(vc.dtype), vc)

    n_sel = s // SEL_BLOCK
    k_top = min(SEL_TOPK, n_sel)
    sel_start = jnp.arange(n_sel) * SEL_BLOCK
    overlap = ((cmp_start[:, None] <= sel_start[None, :] + SEL_BLOCK - 1)
               & (cmp_end[:, None] >= sel_start[None, :])).astype(jnp.float32)
    imp = jnp.einsum('bhgsn,nj->bhsj', p_cmp, overlap)
    blk_id = jnp.arange(n_sel)[None, :]
    cur = (pos // SEL_BLOCK)[:, None]
    valid = sel_start[None, :] <= pos[:, None]
    forced = (blk_id == 0) | (blk_id == cur) | (blk_id == cur - 1)
    imp = jnp.where(forced & valid, jnp.inf, imp)
    imp = jnp.where(valid, imp, -jnp.inf)
    top_val, top_idx = lax.top_k(imp, k_top)
    top_ok = top_val > -jnp.inf
    k_blocks = jnp.transpose(k_sel.reshape(b, n_sel, SEL_BLOCK, NSA_KV_GROUPS, NSA_HEAD_DIM), (0, 3, 1, 2, 4))
    v_blocks = jnp.transpose(v_sel.reshape(b, n_sel, SEL_BLOCK, NSA_KV_GROUPS, NSA_HEAD_DIM), (0, 3, 1, 2, 4))
    bi = jnp.arange(b)[:, None, None, None]
    hi = jnp.arange(NSA_KV_GROUPS)[None, :, None, None]

    def sel_block(i):
        s0 = i * SEL_Q_BLOCK
        qb = lax.dynamic_slice_in_dim(q, s0, SEL_Q_BLOCK, axis=1)
        ib = lax.dynamic_slice_in_dim(top_idx, s0, SEL_Q_BLOCK, axis=2)
        okb = lax.dynamic_slice_in_dim(top_ok, s0, SEL_Q_BLOCK, axis=2)
        kg = k_blocks[bi, hi, ib]
        vg = v_blocks[bi, hi, ib]
        sc_s = jnp.einsum('bqhgd,bhqkld->bhgqkl', qb, kg) * scale
        kpos = ib[..., None] * SEL_BLOCK + jnp.arange(SEL_BLOCK)
        tq = s0 + jnp.arange(SEL_Q_BLOCK)
        mask = (kpos <= tq[None, None, :, None, None]) & okb[..., None]
        flat = k_top * SEL_BLOCK
        p = masked_softmax(sc_s.reshape(b, NSA_KV_GROUPS, NSA_GROUP, SEL_Q_BLOCK, flat),
                           mask.reshape(b, NSA_KV_GROUPS, 1, SEL_Q_BLOCK, flat))
        p = p.reshape(b, NSA_KV_GROUPS, NSA_GROUP, SEL_Q_BLOCK, k_top, SEL_BLOCK).astype(vg.dtype)
        return jnp.einsum('bhgqkl,bhqkld->bqhgd', p, vg)

    o_sel = lax.map(sel_block, jnp.arange(s // SEL_Q_BLOCK))
    o_sel = jnp.moveaxis(o_sel, 0, 1).reshape(q.shape)

    kw_pad = jnp.pad(k_win, ((0, 0), (WINDOW, 0), (0, 0), (0, 0)))
    vw_pad = jnp.pad(v_win, ((0, 0), (WINDOW, 0), (0, 0), (0, 0)))

    def win_block(i):
        s0 = i * Q_BLOCK
        qb = lax.dynamic_slice_in_dim(q, s0, Q_BLOCK, axis=1)
        kb = lax.dynamic_slice_in_dim(kw_pad, s0, WINDOW + Q_BLOCK, axis=1)
        vb = lax.dynamic_slice_in_dim(vw_pad, s0, WINDOW + Q_BLOCK, axis=1)
        sc_w = jnp.einsum('bqhgd,bkhd->bhgqk', qb, kb) * scale
        kpos = s0 - WINDOW + jnp.arange(WINDOW + Q_BLOCK)
        tq = s0 + jnp.arange(Q_BLOCK)
        mask = (kpos[None, :] >= 0) & (kpos[None, :] <= tq[:, None]) & (tq[:, None] - kpos[None, :] < WINDOW)
        p = masked_softmax(sc_w, mask).astype(vb.dtype)
        return jnp.einsum('bhgqk,bkhd->bqhgd', p, vb)

    o_win = lax.map(win_block, jnp.arange(s // Q_BLOCK))
    o_win = jnp.moveaxis(o_win, 0, 1).reshape(q.shape)

    o = gates[..., 0:1] * o_cmp + gates[..., 1:2] * o_sel + gates[..., 2:3] * o_win
    return o.reshape(b, s, NSA_Q_WIDTH)


def mla_attention(c_q, c_kv, k_rope_raw, q_norm, w_uq, kv_norm, w_uk, w_uv, cos, sin):
    b, s, _ = c_q.shape
    q = (rms_norm(c_q, q_norm) @ w_uq).reshape(b, s, MLA_HEADS, MLA_QK_DIM)
    q_nope = q[..., :MLA_NOPE_DIM]
    q_rope = apply_partial_rope(q[..., MLA_NOPE_DIM:], cos, sin, MLA_ROPE_DIM)
    c = rms_norm(c_kv, kv_norm)
    k_nope = (c @ w_uk).reshape(b, s, MLA_HEADS, MLA_NOPE_DIM)
    v = (c @ w_uv).reshape(b, s, MLA_HEADS, MLA_V_DIM)
    k_rope = apply_partial_rope(k_rope_raw[:, :, None, :], cos, sin, MLA_ROPE_DIM)[:, :, 0]
    scale = MLA_QK_DIM ** -0.5
    kpos = jnp.arange(s)

    def block(i):
        s0 = i * Q_BLOCK
        qn = lax.dynamic_slice_in_dim(q_nope, s0, Q_BLOCK, axis=1)
        qr = lax.dynamic_slice_in_dim(q_rope, s0, Q_BLOCK, axis=1)
        sc = (jnp.einsum('bqhd,bkhd->bhqk', qn, k_nope) + jnp.einsum('bqhd,bkd->bhqk', qr, k_rope)) * scale
        tq = s0 + jnp.arange(Q_BLOCK)
        p = masked_softmax(sc, kpos[None, :] <= tq[:, None]).astype(v.dtype)
        return jnp.einsum('bhqk,bkhd->bqhd', p, v)

    o = lax.map(block, jnp.arange(s // Q_BLOCK))
    return jnp.moveaxis(o, 0, 1).reshape(b, s, MLA_HEADS * MLA_V_DIM)


def conv_glu(x, w_gate, w_up, conv_w, conv_b, w_down):
    a = x @ w_gate
    a = lax.conv_general_dilated(a, conv_w[:, None, :], window_strides=(1,), padding=[(CONV_WIDTH - 1, 0)],
                                 dimension_numbers=('NWC', 'WIO', 'NWC'), feature_group_count=D_FF) + conv_b
    return (jax.nn.gelu(a) * (x @ w_up)) @ w_down


def setup_inputs(seed: int = 0) -> dict:
    key = jax.random.key(seed)
    ks = jax.random.split(key, 27)

    def nrm(k, shape, scale):
        return jax.random.normal(k, (DEPTH,) + shape, jnp.float32) * scale

    def gain(k, n):
        return 1.0 + 0.02 * jax.random.normal(k, (DEPTH, n), jnp.float32)

    dk = NSA_HEAD_DIM
    flat = CMP_BLOCK * dk
    return {
        'x': jax.random.normal(ks[0], (BATCH, SEQ, D_MODEL), jnp.float32),
        'w_in': nrm(ks[1], (D_MODEL, IN_TOTAL), D_MODEL ** -0.5),
        'cmp_pe_k': nrm(ks[2], (CMP_BLOCK, dk), 0.1),
        'cmp_pe_v': nrm(ks[3], (CMP_BLOCK, dk), 0.1),
        'cmp_k_w1': nrm(ks[4], (flat, CMP_HIDDEN), flat ** -0.5),
        'cmp_k_b1': nrm(ks[5], (CMP_HIDDEN,), 0.02),
        'cmp_k_w2': nrm(ks[6], (CMP_HIDDEN, dk), CMP_HIDDEN ** -0.5),
        'cmp_v_w1': nrm(ks[7], (flat, CMP_HIDDEN), flat ** -0.5),
        'cmp_v_b1': nrm(ks[8], (CMP_HIDDEN,), 0.02),
        'cmp_v_w2': nrm(ks[9], (CMP_HIDDEN, dk), CMP_HIDDEN ** -0.5),
        'nsa_w_o': nrm(ks[10], (NSA_Q_WIDTH, D_MODEL), NSA_Q_WIDTH ** -0.5),
        'mla_q_norm': gain(ks[11], MLA_Q_RANK),
        'mla_w_uq': nrm(ks[12], (MLA_Q_RANK, MLA_HEADS * MLA_QK_DIM), MLA_Q_RANK ** -0.5),
        'mla_kv_norm': gain(ks[13], MLA_KV_RANK),
        'mla_w_uk': nrm(ks[14], (MLA_KV_RANK, MLA_HEADS * MLA_NOPE_DIM), MLA_KV_RANK ** -0.5),
        'mla_w_uv': nrm(ks[15], (MLA_KV_RANK, MLA_HEADS * MLA_V_DIM), MLA_KV_RANK ** -0.5),
        'mla_w_o': nrm(ks[16], (MLA_HEADS * MLA_V_DIM, D_MODEL), (MLA_HEADS * MLA_V_DIM) ** -0.5),
        'w_out': nrm(ks[17], (D_MODEL, D_MODEL), DEEPNORM_BETA * D_MODEL ** -0.5),
        'ln1_g': gain(ks[18], D_MODEL),
        'ln1_b': nrm(ks[19], (D_MODEL,), 0.02),
        'ffn_w_gate': nrm(ks[20], (D_MODEL, D_FF), D_MODEL ** -0.5),
        'ffn_w_up': nrm(ks[21], (D_MODEL, D_FF), D_MODEL ** -0.5),
        'ffn_conv_w': nrm(ks[22], (CONV_WIDTH, D_FF), CONV_WIDTH ** -0.5),
        'ffn_conv_b': nrm(ks[23], (D_FF,), 0.02),
        'ffn_w_down': nrm(ks[24], (D_FF, D_MODEL), DEEPNORM_BETA * D_FF ** -0.5),
        'ln2_g': gain(ks[25], D_MODEL),
        'ln2_b': nrm(ks[26], (D_MODEL,), 0.02),
    }


def reference(x, w_in, cmp_pe_k, cmp_pe_v, cmp_k_w1, cmp_k_b1, cmp_k_w2, cmp_v_w1, cmp_v_b1, cmp_v_w2,
              nsa_w_o, mla_q_norm, mla_w_uq, mla_kv_norm, mla_w_uk, mla_w_uv, mla_w_o, w_out,
              ln1_g, ln1_b, ffn_w_gate, ffn_w_up, ffn_conv_w, ffn_conv_b, ffn_w_down, ln2_g, ln2_b):
    b, s, _ = x.shape
    cos_n, sin_n = rope_tables(NSA_ROT_DIM, s)
    cos_m, sin_m = rope_tables(MLA_ROPE_DIM, s)
    for l in range(DEPTH):
        h = x @ w_in[l]
        (nq, kc, vc, ksl, vsl, kw, vw, ng, cq, ckv, kr, mg) = jnp.split(h, IN_OFFSETS, axis=-1)

        kvs = (b, s, NSA_KV_GROUPS, NSA_HEAD_DIM)
        q = apply_partial_rope(nq.reshape(b, s, NSA_HEADS, NSA_HEAD_DIM), cos_n, sin_n, NSA_ROT_DIM)
        q = q.reshape(b, s, NSA_KV_GROUPS, NSA_GROUP, NSA_HEAD_DIM)
        kc = apply_partial_rope(kc.reshape(kvs), cos_n, sin_n, NSA_ROT_DIM)
        ksl = apply_partial_rope(ksl.reshape(kvs), cos_n, sin_n, NSA_ROT_DIM)
        kw = apply_partial_rope(kw.reshape(kvs), cos_n, sin_n, NSA_ROT_DIM)
        gates = jax.nn.sigmoid(ng.reshape(b, s, NSA_KV_GROUPS, NSA_GROUP, 3))
        o_nsa = nsa_attention(q, kc, vc.reshape(kvs), ksl, vsl.reshape(kvs), kw, vw.reshape(kvs), gates,
                              cmp_pe_k[l], cmp_pe_v[l], cmp_k_w1[l], cmp_k_b1[l], cmp_k_w2[l],
                              cmp_v_w1[l], cmp_v_b1[l], cmp_v_w2[l])
        y_a = o_nsa @ nsa_w_o[l]

        o_mla = mla_attention(cq, ckv, kr, mla_q_norm[l], mla_w_uq[l], mla_kv_norm[l], mla_w_uk[l], mla_w_uv[l],
                              cos_m, sin_m)
        y_b = o_mla @ mla_w_o[l]

        g = jax.nn.sigmoid(mg.reshape(b, s, 2, D_MODEL))
        mix = (g[:, :, 0] * y_a + g[:, :, 1] * y_b) @ w_out[l]
        x = layer_norm(DEEPNORM_ALPHA * x + mix, ln1_g[l], ln1_b[l])

        f = conv_glu(x, ffn_w_gate[l], ffn_w_up[l], ffn_conv_w[l], ffn_conv_b[l], ffn_w_down[l])
        x = layer_norm(DEEPNORM_ALPHA * x + f, ln2_g[l], ln2_b[l])
    return x
```

```python
import functools

import numpy as np
import jax
import jax.numpy as jnp
from jax import lax
from jax.experimental import pallas as pl
from jax.experimental.pallas import tpu as pltpu

D_MODEL = 1024
ROPE_THETA = 500000.0
NSA_HEADS = 8
NSA_KV_GROUPS = 2
NSA_GROUP = 4
NSA_HEAD_DIM = 64
NSA_ROT_DIM = 16
CMP_BLOCK = 32
CMP_STRIDE = 16
CMP_HIDDEN = 256
SEL_BLOCK = 64
SEL_TOPK = 8
WINDOW = 256
MLA_HEADS = 8
MLA_Q_RANK = 768
MLA_KV_RANK = 256
MLA_NOPE_DIM = 64
MLA_ROPE_DIM = 32
MLA_V_DIM = 64
MLA_QK_DIM = 96
D_FF = 2816
LN_EPS = 1e-5
RMS_EPS = 1e-6
DEPTH = 1
DEEPNORM_ALPHA = (2 * DEPTH) ** 0.25

LANES = 128
HALF = LANES // 2
NEG = -1e30
BIG = 3e38
VMEM_LIMIT = 56 * 1024 * 1024

TM_PROJ = 512
TQ_NSA = 128
TK_SEL = 512
TQ_MLA = 256
TK_MLA = 512
TM_MERGE = 512
TM_FFN = 512
FC_FFN = 256

BF16 = jnp.bfloat16
F32 = jnp.float32


def _dot(a, b):
    return jnp.dot(a, b, preferred_element_type=F32)


def _dot_nt(a, b):
    return lax.dot_general(a, b, (((1,), (1,)), ((), ())), preferred_element_type=F32)


def _gelu_tanh(x):
    return x * (0.5 * (1.0 + jnp.tanh(0.7978845608028654 * (x + 0.044715 * (x * x * x)))))


def _sigmoid(x):
    return 1.0 / (1.0 + jnp.exp(-x))


def _layer_norm(x, g, b):
    mu = jnp.mean(x, axis=-1, keepdims=True)
    xc = x - mu
    var = jnp.mean(xc * xc, axis=-1, keepdims=True)
    return xc * lax.rsqrt(var + LN_EPS) * g + b


def _rms_norm(x, g):
    return x * lax.rsqrt(jnp.mean(x * x, axis=-1, keepdims=True) + RMS_EPS) * g


def _rope_chunk(x, cos, sin_signed, first_half, half):
    partner = jnp.where(first_half, pltpu.roll(x, LANES - half, 1), pltpu.roll(x, half, 1))
    return x * cos + partner * sin_signed


def _masked_softmax(s, mask):
    sm = jnp.where(mask, s, NEG)
    m = jnp.max(sm, axis=-1, keepdims=True)
    e = jnp.where(mask, jnp.exp(sm - m), 0.0)
    return e / jnp.maximum(jnp.sum(e, axis=-1, keepdims=True), 1e-30)


def _proj_kernel(x_ref, tab_ref, wq_ref, wkv_ref, wg_ref, wcq_ref, wckv_ref, wkr_ref,
                 qn_ref, kvn_ref, wuq_ref, wuk_ref, wuv_ref, place_ref,
                 qnsa_ref, kcmp_ref, vcmp_ref, kvsw_ref, gates_ref, qmla_ref, kmla_ref, vmla_ref):
    xb = x_ref[...].astype(BF16)
    tm = xb.shape[0]
    lane = lax.broadcasted_iota(jnp.int32, (tm, LANES), 1)
    nsa_first = (lane % HALF) < (NSA_ROT_DIM // 2)
    mlaq_first = lane < (MLA_NOPE_DIM + MLA_ROPE_DIM // 2)
    mlak_first = lane < (MLA_ROPE_DIM // 2)
    cn, sn = tab_ref[0], tab_ref[1]
    cq_t, sq_t = tab_ref[2], tab_ref[3]
    ck_t, sk_t = tab_ref[4], tab_ref[5]

    def rope_n(v):
        return _rope_chunk(v, cn, sn, nsa_first, NSA_ROT_DIM // 2)

    q = _dot(xb, wq_ref[...])
    for c in range(4):
        sl = slice(c * LANES, (c + 1) * LANES)
        qnsa_ref[:, sl] = (rope_n(q[:, sl]) * (NSA_HEAD_DIM ** -0.5)).astype(BF16)

    kv = _dot(xb, wkv_ref[...])
    kcmp_ref[...] = rope_n(kv[:, 0:128])
    vcmp_ref[...] = kv[:, 128:256]
    kvsw_ref[:, 0:128] = rope_n(kv[:, 256:384]).astype(BF16)
    kvsw_ref[:, 128:256] = kv[:, 384:512].astype(BF16)
    kvsw_ref[:, 256:384] = rope_n(kv[:, 512:640]).astype(BF16)
    kvsw_ref[:, 384:512] = kv[:, 640:768].astype(BF16)

    gates_ref[...] = _sigmoid(_dot(xb, wg_ref[...]))

    cq = _rms_norm(_dot(xb, wcq_ref[...]), qn_ref[...]).astype(BF16)
    qm = _dot(cq, wuq_ref[...])
    for h in range(MLA_HEADS):
        sl = slice(h * LANES, (h + 1) * LANES)
        qmla_ref[:, sl] = _rope_chunk(qm[:, sl], cq_t, sq_t, mlaq_first, MLA_ROPE_DIM // 2).astype(BF16)

    ckv = _rms_norm(_dot(xb, wckv_ref[...]), kvn_ref[...]).astype(BF16)
    kr = _rope_chunk(_dot(xb, wkr_ref[...]), ck_t, sk_t, mlak_first, MLA_ROPE_DIM // 2).astype(BF16)
    kmla_ref[...] = (_dot(ckv, wuk_ref[...]) + _dot(kr, place_ref[...])).astype(BF16)
    vmla_ref[...] = _dot(ckv, wuv_ref[...]).astype(BF16)


def _compress_kernel(kc_ref, vc_ref, pek_ref, pev_ref, wka_ref, wkb_ref, wva_ref, wvb_ref,
                     bk_ref, bv_ref, wk2_ref, wv2_ref, kout_ref, vout_ref):
    def one(tok_ref, pe_ref, wa_ref, wb_ref, b_ref, w2_ref, out_ref):
        ch = tok_ref[...]
        nxt = pltpu.roll(ch, ch.shape[0] - 1, 0)
        a = (ch + pe_ref[0:1, :]).astype(BF16)
        b = (nxt + pe_ref[1:2, :]).astype(BF16)
        h = _dot(a, wa_ref[...]) + _dot(b, wb_ref[...]) + b_ref[...]
        out_ref[...] = _dot(_gelu_tanh(h).astype(BF16), w2_ref[...]).astype(BF16)

    one(kc_ref, pek_ref, wka_ref, wkb_ref, bk_ref, wk2_ref, kout_ref)
    one(vc_ref, pev_ref, wva_ref, wvb_ref, bv_ref, wv2_ref, vout_ref)


def _nsa_kernel(q_ref, kvsw_ref, kc_ref, vc_ref, gates_ref, expand_ref, overlap_ref, o_ref,
                m_scr, l_scr, acc_scr, *, seq):
    tq = TQ_NSA
    rows = 8 * tq
    i = pl.program_id(1)
    s0 = i * tq
    q = q_ref[...]
    lane_q = lax.broadcasted_iota(jnp.int32, (tq, LANES), 1)
    lo = lane_q < HALF
    zero = jnp.zeros((tq, LANES), BF16)
    blocks = []
    for half in range(2):
        for c in range(4):
            chunk = q[:, c * LANES:(c + 1) * LANES]
            blocks.append(jnp.where(lo if half == 0 else jnp.logical_not(lo), chunk, zero))
    qs = jnp.concatenate(blocks, axis=0)

    def row_pos(width):
        r = lax.broadcasted_iota(jnp.int32, (rows, width), 0)
        return s0 + (r & (tq - 1))

    ncmp = kc_ref.shape[0]
    sc = _dot_nt(qs, kc_ref[...])
    n_idx = lax.broadcasted_iota(jnp.int32, (rows, ncmp), 1)
    p_cmp = _masked_softmax(sc, n_idx * CMP_STRIDE + (CMP_BLOCK - 1) <= row_pos(ncmp))
    o_cmp = _dot(p_cmp.astype(BF16), vc_ref[...])

    pos_q = s0 + lax.broadcasted_iota(jnp.int32, (tq, LANES), 0)
    cur = pos_q // SEL_BLOCK
    valid = lane_q <= cur
    forced = (lane_q == 0) | (lane_q == cur) | (lane_q == cur - 1)
    sel = []
    for g in range(2):
        psum = p_cmp[(4 * g) * tq:(4 * g + 1) * tq]
        for r in range(4 * g + 1, 4 * g + 4):
            psum = psum + p_cmp[r * tq:(r + 1) * tq]
        hi = psum.astype(BF16)
        lo_part = (psum - hi.astype(F32)).astype(BF16)
        imp = _dot(hi, overlap_ref[...]) + _dot(lo_part, overlap_ref[...])
        impv = jnp.where(valid, jnp.where(forced, BIG, imp), -1.0)
        cnt = jnp.zeros((tq, LANES), F32)
        for b in range(seq // SEL_BLOCK):
            col = impv[:, b:b + 1]
            beats = (col > impv) | ((col == impv) & (lane_q > b))
            cnt = cnt + jnp.where(beats, 1.0, 0.0)
        sel.append(jnp.where(valid & (cnt < float(SEL_TOPK)), 1.0, 0.0).astype(BF16))

    m_scr[...] = jnp.full((rows, 1), NEG, F32)
    l_scr[...] = jnp.zeros((rows, 1), F32)
    acc_scr[...] = jnp.zeros((rows, LANES), F32)
    pos_k = row_pos(TK_SEL)
    k_iota = lax.broadcasted_iota(jnp.int32, (rows, TK_SEL), 1)

    def sel_step(kt, carry):
        k0 = pl.multiple_of(kt * TK_SEL, TK_SEL)
        k = kvsw_ref[pl.ds(k0, TK_SEL), 0:128]
        v = kvsw_ref[pl.ds(k0, TK_SEL), 128:256]
        s = _dot_nt(qs, k)
        ex = expand_ref[kt]
        m0 = _dot(sel[0], ex)
        m1 = _dot(sel[1], ex)
        msel = jnp.concatenate([m0, m0, m0, m0, m1, m1, m1, m1], axis=0)
        mask = (msel > 0.5) & (k0 + k_iota <= pos_k)
        sm = jnp.where(mask, s, NEG)
        m_prev = m_scr[...]
        m_new = jnp.maximum(m_prev, jnp.max(sm, axis=-1, keepdims=True))
        alpha = jnp.exp(m_prev - m_new)
        e = jnp.where(mask, jnp.exp(sm - m_new), 0.0)
        l_scr[...] = alpha * l_scr[...] + jnp.sum(e, axis=-1, keepdims=True)
        acc_scr[...] = alpha * acc_scr[...] + _dot(e.astype(BF16), v)
        m_scr[...] = m_new
        return carry

    lax.fori_loop(0, (s0 + tq + TK_SEL - 1) // TK_SEL, sel_step, 0)
    o_sel = acc_scr[...] / jnp.maximum(l_scr[...], 1e-30)

    wk = WINDOW + tq
    w0 = pl.multiple_of(jnp.maximum(s0 - WINDOW, 0), tq)
    kw = kvsw_ref[pl.ds(w0, wk), 256:384]
    vw = kvsw_ref[pl.ds(w0, wk), 384:512]
    sw = _dot_nt(qs, kw)
    kpos = w0 + lax.broadcasted_iota(jnp.int32, (rows, wk), 1)
    pos_w = row_pos(wk)
    p_win = _masked_softmax(sw, (kpos <= pos_w) & (pos_w - kpos < WINDOW))
    o_win = _dot(p_win.astype(BF16), vw)

    gates = gates_ref[...]
    outs = []
    for r in range(8):
        sl = slice(r * tq, (r + 1) * tq)
        outs.append(gates[:, r:r + 1] * o_cmp[sl] + gates[:, 8 + r:9 + r] * o_sel[sl]
                    + gates[:, 16 + r:17 + r] * o_win[sl])
    for c in range(4):
        o_ref[:, c * LANES:(c + 1) * LANES] = jnp.where(lo, outs[c], outs[4 + c]).astype(BF16)


def _mla_kernel(q_ref, k_ref, v_ref, o_ref, m_scr, l_scr, acc_scr):
    tq = TQ_MLA
    i = pl.program_id(1)
    s0 = i * tq
    scale = MLA_QK_DIM ** -0.5
    m_scr[...] = jnp.full(m_scr.shape, NEG, F32)
    l_scr[...] = jnp.zeros(l_scr.shape, F32)
    acc_scr[...] = jnp.zeros(acc_scr.shape, F32)
    pos_q = s0 + lax.broadcasted_iota(jnp.int32, (tq, TK_MLA), 0)
    k_iota = lax.broadcasted_iota(jnp.int32, (tq, TK_MLA), 1)

    def step(kt, carry):
        k0 = pl.multiple_of(kt * TK_MLA, TK_MLA)
        mask = k0 + k_iota <= pos_q
        for h in range(MLA_HEADS):
            qh = q_ref[:, h * LANES:(h + 1) * LANES]
            kh = k_ref[pl.ds(k0, TK_MLA), h * LANES:(h + 1) * LANES]
            vh = v_ref[pl.ds(k0, TK_MLA), (h // 2) * LANES:(h // 2 + 1) * LANES]
            s = _dot_nt(qh, kh) * scale
            sm = jnp.where(mask, s, NEG)
            m_prev = m_scr[h]
            m_new = jnp.maximum(m_prev, jnp.max(sm, axis=-1, keepdims=True))
            alpha = jnp.exp(m_prev - m_new)
            e = jnp.where(mask, jnp.exp(sm - m_new), 0.0)
            l_scr[h] = alpha * l_scr[h] + jnp.sum(e, axis=-1, keepdims=True)
            acc_scr[h] = alpha * acc_scr[h] + _dot(e.astype(BF16), vh)
            m_scr[h] = m_new
        return carry

    lax.fori_loop(0, (s0 + tq + TK_MLA - 1) // TK_MLA, step, 0)
    lo = lax.broadcasted_iota(jnp.int32, (tq, LANES), 1) < HALF
    for c in range(4):
        oa = acc_scr[2 * c] / jnp.maximum(l_scr[2 * c], 1e-30)
        ob = acc_scr[2 * c + 1] / jnp.maximum(l_scr[2 * c + 1], 1e-30)
        o_ref[:, c * LANES:(c + 1) * LANES] = jnp.where(lo, oa, ob).astype(BF16)


def _merge_kernel(x_ref, onsa_ref, omla_ref, wa_ref, wb_ref, wmg_ref, wout_ref, g_ref, b_ref, y_ref):
    x = x_ref[...]
    ya = _dot(onsa_ref[...], wa_ref[...])
    yb = _dot(omla_ref[...], wb_ref[...])
    mg = _sigmoid(_dot(x.astype(BF16), wmg_ref[...]))
    mixed = (mg[:, :D_MODEL] * ya + mg[:, D_MODEL:] * yb).astype(BF16)
    y_ref[...] = _layer_norm(DEEPNORM_ALPHA * x + _dot(mixed, wout_ref[...]), g_ref[...], b_ref[...])


def _ffn_kernel(x_ref, wg_ref, wu_ref, wd_ref, cw_ref, cb_ref, g_ref, b_ref, y_ref, tail_scr, acc_scr):
    tm = TM_FFN
    first_tile = pl.program_id(1) == 0
    x = x_ref[...]
    xb = x.astype(BF16)
    row = lax.broadcasted_iota(jnp.int32, (tm, FC_FFN), 0)
    acc_scr[...] = jnp.zeros(acc_scr.shape, F32)

    @pl.when(first_tile)
    def _():
        tail_scr[...] = jnp.zeros(tail_scr.shape, F32)

    def chunk(fc, carry):
        a = _dot(xb, wg_ref[fc])
        tail = tail_scr[fc]
        tail_scr[fc] = a[tm - 8:, :]
        a1 = jnp.where(row == 0, tail[7:8, :], pltpu.roll(a, 1, 0))
        a2 = jnp.where(row == 0, tail[6:7, :], jnp.where(row == 1, tail[7:8, :], pltpu.roll(a, 2, 0)))
        cw = cw_ref[fc]
        conv = cw[0:1, :] * a2 + cw[1:2, :] * a1 + cw[2:3, :] * a + cb_ref[fc][0:1, :]
        hidden = (_gelu_tanh(conv) * _dot(xb, wu_ref[fc])).astype(BF16)
        acc_scr[...] += _dot(hidden, wd_ref[fc])
        return carry

    lax.fori_loop(0, D_FF // FC_FFN, chunk, 0)
    y_ref[...] = _layer_norm(DEEPNORM_ALPHA * x + acc_scr[...], g_ref[...], b_ref[...])


def _rope_tables(seq):
    pos = jnp.arange(seq, dtype=F32)

    def cs(rot):
        inv = ROPE_THETA ** (-jnp.arange(0, rot, 2, dtype=F32) / rot)
        ang = pos[:, None] * inv[None, :]
        return jnp.cos(ang), jnp.sin(ang)

    one = lambda n: jnp.ones((seq, n), F32)
    zero = lambda n: jnp.zeros((seq, n), F32)
    c8, s8 = cs(NSA_ROT_DIM)
    c16, s16 = cs(MLA_ROPE_DIM)
    nsa_c = jnp.concatenate([c8, c8, one(48)] * 2, axis=1)
    nsa_s = jnp.concatenate([-s8, s8, zero(48)] * 2, axis=1)
    mq_c = jnp.concatenate([one(64), c16, c16, one(32)], axis=1)
    mq_s = jnp.concatenate([zero(64), -s16, s16, zero(32)], axis=1)
    mk_c = jnp.concatenate([c16, c16, one(96)], axis=1)
    mk_s = jnp.concatenate([-s16, s16, zero(96)], axis=1)
    return jnp.stack([nsa_c, nsa_s, mq_c, mq_s, mk_c, mk_s])


def _head_perm():
    return np.array([(c + 4 * half) * 64 + d for c in range(4) for half in range(2) for d in range(64)])


def _const(shape):
    nd = len(shape)
    return pl.BlockSpec(shape, lambda *_: (0,) * nd)


def _params(n_axes):
    return pltpu.CompilerParams(dimension_semantics=("arbitrary",) * n_axes, vmem_limit_bytes=VMEM_LIMIT)


def kernel(x, w_in, cmp_pe_k, cmp_pe_v, cmp_k_w1, cmp_k_b1, cmp_k_w2, cmp_v_w1, cmp_v_b1, cmp_v_w2,
           nsa_w_o, mla_q_norm, mla_w_uq, mla_kv_norm, mla_w_uk, mla_w_uv, mla_w_o, w_out,
           ln1_g, ln1_b, ffn_w_gate, ffn_w_up, ffn_conv_w, ffn_conv_b, ffn_w_down, ln2_g, ln2_b):
    b, s, d = x.shape
    assert d == D_MODEL and s % TK_SEL == 0 and s % TM_FFN == 0 and (s // CMP_STRIDE) % 8 == 0
    t = b * s
    perm = _head_perm()
    xf = x.reshape(t, d)
    for l in range(DEPTH):
        xf = _layer(xf, b, s, perm, w_in[l], cmp_pe_k[l], cmp_pe_v[l], cmp_k_w1[l], cmp_k_b1[l], cmp_k_w2[l],
                    cmp_v_w1[l], cmp_v_b1[l], cmp_v_w2[l], nsa_w_o[l], mla_q_norm[l], mla_w_uq[l],
                    mla_kv_norm[l], mla_w_uk[l], mla_w_uv[l], mla_w_o[l], w_out[l], ln1_g[l], ln1_b[l],
                    ffn_w_gate[l], ffn_w_up[l], ffn_conv_w[l], ffn_conv_b[l], ffn_w_down[l], ln2_g[l], ln2_b[l])
    return xf.reshape(b, s, d)


def _layer(xf, b, s, perm, w_in, pe_k, pe_v, ck_w1, ck_b1, ck_w2, cv_w1, cv_b1, cv_w2, nsa_w_o, q_norm, w_uq,
           kv_norm, w_uk, w_uv, mla_w_o, w_out, ln1_g, ln1_b, w_gate, w_up, conv_w, conv_b, w_down, ln2_g, ln2_b):
    t = b * s
    d = D_MODEL
    o = np.cumsum([0, 512, 128, 128, 128, 128, 128, 128, 24, MLA_Q_RANK, MLA_KV_RANK, MLA_ROPE_DIM, 2 * d])
    wq = w_in[:, o[0]:o[1]][:, perm].astype(BF16)
    wkv = w_in[:, o[1]:o[7]].astype(BF16)
    gate_cols = np.array([h * 3 + br for br in range(3) for h in range(8)])
    wg = jnp.pad(w_in[:, o[7]:o[8]][:, gate_cols], ((0, 0), (0, LANES - 24))).astype(BF16)
    wcq = w_in[:, o[8]:o[9]].astype(BF16)
    wckv = w_in[:, o[9]:o[10]].astype(BF16)
    wkr = jnp.pad(w_in[:, o[10]:o[11]], ((0, 0), (0, LANES - MLA_ROPE_DIM))).astype(BF16)
    wmg = w_in[:, o[11]:o[12]].astype(BF16)

    wuq = jnp.pad(w_uq.reshape(MLA_Q_RANK, MLA_HEADS, MLA_QK_DIM), ((0, 0), (0, 0), (0, LANES - MLA_QK_DIM)))
    wuq = wuq.reshape(MLA_Q_RANK, MLA_HEADS * LANES).astype(BF16)
    wuk = jnp.pad(w_uk.reshape(MLA_KV_RANK, MLA_HEADS, MLA_NOPE_DIM), ((0, 0), (0, 0), (0, LANES - MLA_NOPE_DIM)))
    wuk = wuk.reshape(MLA_KV_RANK, MLA_HEADS * LANES).astype(BF16)
    place = np.zeros((LANES, MLA_HEADS * LANES), np.float32)
    for h in range(MLA_HEADS):
        for r in range(MLA_ROPE_DIM):
            place[r, h * LANES + MLA_NOPE_DIM + r] = 1.0
    place = jnp.asarray(place, BF16)
    tabs = _rope_tables(s)

    tm = TM_PROJ
    n_s = s // tm
    row_blk = lambda w: pl.BlockSpec((tm, w), lambda i: (i, 0))
    outs = pl.pallas_call(
        _proj_kernel,
        grid=(t // tm,),
        in_specs=[row_blk(d), pl.BlockSpec((6, tm, LANES), lambda i: (0, i % n_s, 0)),
                  _const(wq.shape), _const(wkv.shape), _const(wg.shape), _const(wcq.shape), _const(wckv.shape),
                  _const(wkr.shape), _const((1, MLA_Q_RANK)), _const((1, MLA_KV_RANK)), _const(wuq.shape),
                  _const(wuk.shape), _const((MLA_KV_RANK, 512)), _const(place.shape)],
        out_specs=[row_blk(512), row_blk(128), row_blk(128), row_blk(512), row_blk(128),
                   row_blk(1024), row_blk(1024), row_blk(512)],
        out_shape=[jax.ShapeDtypeStruct((t, 512), BF16), jax.ShapeDtypeStruct((t, 128), F32),
                   jax.ShapeDtypeStruct((t, 128), F32), jax.ShapeDtypeStruct((t, 512), BF16),
                   jax.ShapeDtypeStruct((t, 128), F32), jax.ShapeDtypeStruct((t, 1024), BF16),
                   jax.ShapeDtypeStruct((t, 1024), BF16), jax.ShapeDtypeStruct((t, 512), BF16)],
        compiler_params=_params(1),
        name="proj",
    )(xf, tabs, wq, wkv, wg, wcq, wckv, wkr, q_norm.reshape(1, -1), kv_norm.reshape(1, -1), wuq, wuk,
      w_uv.astype(BF16), place)
    qnsa, kcmp, vcmp, kvsw, gates, qmla, kmla, vmla = outs

    nch = s // CMP_STRIDE
    flat = CMP_STRIDE * LANES

    def big_w1(w1):
        w = w1.reshape(2, CMP_STRIDE, NSA_HEAD_DIM, CMP_HIDDEN)
        eye = jnp.eye(2, dtype=F32)
        return jnp.einsum('hldj,ge->hlgdej', w, eye).reshape(2, flat, 2 * CMP_HIDDEN).astype(BF16)

    def big_pe(pe):
        return jnp.tile(pe.reshape(2, CMP_STRIDE, 1, NSA_HEAD_DIM), (1, 1, 2, 1)).reshape(2, flat)

    def big_w2(w2):
        return jnp.einsum('jd,ge->gjed', w2, jnp.eye(2, dtype=F32)).reshape(2 * CMP_HIDDEN, LANES).astype(BF16)

    wk1, wv1 = big_w1(ck_w1), big_w1(cv_w1)
    bk = jnp.tile(ck_b1, 2).reshape(1, -1)
    bv = jnp.tile(cv_b1, 2).reshape(1, -1)
    tok_blk = pl.BlockSpec((None, nch, flat), lambda i: (i, 0, 0))
    cmp_blk = pl.BlockSpec((None, nch, LANES), lambda i: (i, 0, 0))
    kc, vc = pl.pallas_call(
        _compress_kernel,
        grid=(b,),
        in_specs=[tok_blk, tok_blk, _const((2, flat)), _const((2, flat)),
                  _const((flat, 512)), _const((flat, 512)), _const((flat, 512)), _const((flat, 512)),
                  _const((1, 512)), _const((1, 512)), _const((512, LANES)), _const((512, LANES))],
        out_specs=[cmp_blk, cmp_blk],
        out_shape=[jax.ShapeDtypeStruct((b, nch, LANES), BF16)] * 2,
        compiler_params=_params(1),
        name="compress",
    )(kcmp.reshape(b, nch, flat), vcmp.reshape(b, nch, flat), big_pe(pe_k), big_pe(pe_v),
      wk1[0], wk1[1], wv1[0], wv1[1], bk, bv, big_w2(ck_w2), big_w2(cv_w2))

    n_sel = s // SEL_BLOCK
    key_blk = np.arange(s) // SEL_BLOCK
    expand = (np.arange(LANES)[:, None] == key_blk[None, :]).astype(np.float32)
    expand = jnp.asarray(expand.reshape(LANES, s // TK_SEL, TK_SEL).transpose(1, 0, 2), BF16)
    n_i = np.arange(nch)[:, None]
    j_i = np.arange(LANES)[None, :]
    overlap = ((n_i <= 4 * j_i + 3) & (n_i >= 4 * j_i - 1) & (j_i < n_sel) & (n_i < nch - 1)).astype(np.float32)
    overlap = jnp.asarray(overlap, BF16)
    tq = TQ_NSA
    nq = s // tq
    onsa = pl.pallas_call(
        functools.partial(_nsa_kernel, seq=s),
        grid=(b, nq),
        in_specs=[pl.BlockSpec((tq, 512), lambda bi, i: (bi * nq + i, 0)),
                  pl.BlockSpec((s, 512), lambda bi, i: (bi, 0)),
                  pl.BlockSpec((None, nch, LANES), lambda bi, i: (bi, 0, 0)),
                  pl.BlockSpec((None, nch, LANES), lambda bi, i: (bi, 0, 0)),
                  pl.BlockSpec((tq, LANES), lambda bi, i: (bi * nq + i, 0)),
                  _const(expand.shape), _const(overlap.shape)],
        out_specs=pl.BlockSpec((tq, 512), lambda bi, i: (bi * nq + i, 0)),
        out_shape=jax.ShapeDtypeStruct((t, 512), BF16),
        scratch_shapes=[pltpu.VMEM((8 * tq, 1), F32), pltpu.VMEM((8 * tq, 1), F32),
                        pltpu.VMEM((8 * tq, LANES), F32)],
        compiler_params=_params(2),
        name="nsa",
    )(qnsa, kvsw, kc, vc, gates, expand, overlap)

    tq = TQ_MLA
    nq = s // tq
    omla = pl.pallas_call(
        _mla_kernel,
        grid=(b, nq),
        in_specs=[pl.BlockSpec((tq, 1024), lambda bi, i: (bi * nq + i, 0)),
                  pl.BlockSpec((s, 1024), lambda bi, i: (bi, 0)),
                  pl.BlockSpec((s, 512), lambda bi, i: (bi, 0))],
        out_specs=pl.BlockSpec((tq, 512), lambda bi, i: (bi * nq + i, 0)),
        out_shape=jax.ShapeDtypeStruct((t, 512), BF16),
        scratch_shapes=[pltpu.VMEM((MLA_HEADS, tq, 1), F32), pltpu.VMEM((MLA_HEADS, tq, 1), F32),
                        pltpu.VMEM((MLA_HEADS, tq, LANES), F32)],
        compiler_params=_params(2),
        name="mla",
    )(qmla, kmla, vmla)

    tm = TM_MERGE
    row_blk = lambda w: pl.BlockSpec((tm, w), lambda i: (i, 0))
    x1 = pl.pallas_call(
        _merge_kernel,
        grid=(t // tm,),
        in_specs=[row_blk(d), row_blk(512), row_blk(512), _const((512, d)), _const((512, d)),
                  _const((d, 2 * d)), _const((d, d)), _const((1, d)), _const((1, d))],
        out_specs=row_blk(d),
        out_shape=jax.ShapeDtypeStruct((t, d), F32),
        compiler_params=_params(1),
        name="merge",
    )(xf, onsa, omla, nsa_w_o[perm, :].astype(BF16), mla_w_o.astype(BF16), wmg, w_out.astype(BF16),
      ln1_g.reshape(1, -1), ln1_b.reshape(1, -1))

    tm = TM_FFN
    nfc = D_FF // FC_FFN
    n_s = s // tm
    wg3 = w_gate.reshape(d, nfc, FC_FFN).transpose(1, 0, 2).astype(BF16)
    wu3 = w_up.reshape(d, nfc, FC_FFN).transpose(1, 0, 2).astype(BF16)
    wd3 = w_down.reshape(nfc, FC_FFN, d).astype(BF16)
    cw3 = jnp.pad(conv_w, ((0, 5), (0, 0))).reshape(8, nfc, FC_FFN).transpose(1, 0, 2)
    cb3 = jnp.broadcast_to(conv_b.reshape(nfc, 1, FC_FFN), (nfc, 8, FC_FFN))
    blk = pl.BlockSpec((tm, d), lambda bi, i: (bi * n_s + i, 0))
    y = pl.pallas_call(
        _ffn_kernel,
        grid=(b, n_s),
        in_specs=[blk, _const(wg3.shape), _const(wu3.shape), _const(wd3.shape), _const(cw3.shape),
                  _const(cb3.shape), _const((1, d)), _const((1, d))],
        out_specs=blk,
        out_shape=jax.ShapeDtypeStruct((t, d), F32),
        scratch_shapes=[pltpu.VMEM((nfc, 8, FC_FFN), F32), pltpu.VMEM((tm, d), F32)],
        compiler_params=_params(2),
        name="ffn",
    )(x1, wg3, wu3, wd3, cw3, cb3, ln2_g.reshape(1, -1), ln2_b.reshape(1, -1))
    return y
```

```python
import functools

import numpy as np
import jax
import jax.numpy as jnp
from jax import lax
from jax.experimental import pallas as pl
from jax.experimental.pallas import tpu as pltpu

D_MODEL = 1024
ROPE_THETA = 500000.0
NSA_HEADS = 8
NSA_KV_GROUPS = 2
NSA_GROUP = 4
NSA_HEAD_DIM = 64
NSA_ROT_DIM = 16
CMP_BLOCK = 32
CMP_STRIDE = 16
CMP_HIDDEN = 256
SEL_BLOCK = 64
SEL_TOPK = 8
WINDOW = 256
MLA_HEADS = 8
MLA_Q_RANK = 768
MLA_KV_RANK = 256
MLA_NOPE_DIM = 64
MLA_ROPE_DIM = 32
MLA_V_DIM = 64
MLA_QK_DIM = 96
D_FF = 2816
LN_EPS = 1e-5
RMS_EPS = 1e-6
DEPTH = 1
DEEPNORM_ALPHA = (2 * DEPTH) ** 0.25
LOG2E = 1.4426950408889634

LANES = 128
HALF = LANES // 2
NEG = -1e30
BIG = 3e38
VMEM_LIMIT = 56 * 1024 * 1024

TM_PROJ = 512
TQ_NSA = 128
TK_SEL = 512
TQ_MLA = 256
TK_MLA = 512
TM_MERGE = 512
TM_FFN = 512
FC_FFN = 256

BF16 = jnp.bfloat16
F32 = jnp.float32


def _dot(a, b):
    return jnp.dot(a, b, preferred_element_type=F32)


def _dot_nt(a, b):
    return lax.dot_general(a, b, (((1,), (1,)), ((), ())), preferred_element_type=F32)


def _gelu_tanh(x):
    return x * (0.5 * (1.0 + jnp.tanh(0.7978845608028654 * (x + 0.044715 * (x * x * x)))))


def _sigmoid(x):
    return 1.0 / (1.0 + jnp.exp(-x))


def _layer_norm(x, g, b):
    mu = jnp.mean(x, axis=-1, keepdims=True)
    xc = x - mu
    var = jnp.mean(xc * xc, axis=-1, keepdims=True)
    return xc * lax.rsqrt(var + LN_EPS) * g + b


def _rms_norm(x, g):
    return x * lax.rsqrt(jnp.mean(x * x, axis=-1, keepdims=True) + RMS_EPS) * g


def _rope_chunk(x, cos, sin_signed, first_half, half):
    partner = jnp.where(first_half, pltpu.roll(x, LANES - half, 1), pltpu.roll(x, half, 1))
    return x * cos + partner * sin_signed


def _softmax_rows(s):
    e = jnp.exp2(s - jnp.max(s, axis=-1, keepdims=True))
    return e / jnp.sum(e, axis=-1, keepdims=True)


def _proj_kernel(x_ref, tab_ref, wq_ref, wkv_ref, wg_ref, wcq_ref, wckv_ref, wkr_ref,
                 qn_ref, kvn_ref, wuq_ref, wuk_ref, wuv_ref, place_ref,
                 qnsa_ref, kcmp_ref, vcmp_ref, kvsw_ref, gates_ref, qmla_ref, kmla_ref, vmla_ref):
    xb = x_ref[...].astype(BF16)
    tm = xb.shape[0]
    lane = lax.broadcasted_iota(jnp.int32, (tm, LANES), 1)
    nsa_first = (lane % HALF) < (NSA_ROT_DIM // 2)
    mlaq_first = lane < (MLA_NOPE_DIM + MLA_ROPE_DIM // 2)
    mlak_first = lane < (MLA_ROPE_DIM // 2)
    cn, sn = tab_ref[0], tab_ref[1]
    cq_t, sq_t = tab_ref[2], tab_ref[3]
    ck_t, sk_t = tab_ref[4], tab_ref[5]

    def rope_n(v):
        return _rope_chunk(v, cn, sn, nsa_first, NSA_ROT_DIM // 2)

    q = _dot(xb, wq_ref[...])
    for c in range(4):
        sl = slice(c * LANES, (c + 1) * LANES)
        qnsa_ref[:, sl] = (rope_n(q[:, sl]) * (NSA_HEAD_DIM ** -0.5 * LOG2E)).astype(BF16)

    kv = _dot(xb, wkv_ref[...])
    kcmp_ref[...] = rope_n(kv[:, 0:128])
    vcmp_ref[...] = kv[:, 128:256]
    kvsw_ref[:, 0:128] = rope_n(kv[:, 256:384]).astype(BF16)
    kvsw_ref[:, 128:256] = kv[:, 384:512].astype(BF16)
    kvsw_ref[:, 256:384] = rope_n(kv[:, 512:640]).astype(BF16)
    kvsw_ref[:, 384:512] = kv[:, 640:768].astype(BF16)

    gates_ref[...] = _sigmoid(_dot(xb, wg_ref[...]))

    cq = _rms_norm(_dot(xb, wcq_ref[...]), qn_ref[...]).astype(BF16)
    qm = _dot(cq, wuq_ref[...])
    for h in range(MLA_HEADS):
        sl = slice(h * LANES, (h + 1) * LANES)
        roped = _rope_chunk(qm[:, sl], cq_t, sq_t, mlaq_first, MLA_ROPE_DIM // 2)
        qmla_ref[:, sl] = (roped * (MLA_QK_DIM ** -0.5 * LOG2E)).astype(BF16)

    ckv = _rms_norm(_dot(xb, wckv_ref[...]), kvn_ref[...]).astype(BF16)
    kr = _rope_chunk(_dot(xb, wkr_ref[...]), ck_t, sk_t, mlak_first, MLA_ROPE_DIM // 2).astype(BF16)
    kmla_ref[...] = (_dot(ckv, wuk_ref[...]) + _dot(kr, place_ref[...])).astype(BF16)
    vmla_ref[...] = _dot(ckv, wuv_ref[...]).astype(BF16)


def _compress_kernel(kc_ref, vc_ref, pek_ref, pev_ref, wka_ref, wkb_ref, wva_ref, wvb_ref,
                     bk_ref, bv_ref, wk2_ref, wv2_ref, kout_ref, vout_ref):
    def one(tok_ref, pe_ref, wa_ref, wb_ref, b_ref, w2_ref, out_ref):
        ch = tok_ref[...]
        nxt = pltpu.roll(ch, ch.shape[0] - 1, 0)
        a = (ch + pe_ref[0:1, :]).astype(BF16)
        b = (nxt + pe_ref[1:2, :]).astype(BF16)
        h = _dot(a, wa_ref[...]) + _dot(b, wb_ref[...]) + b_ref[...]
        out_ref[...] = _dot(_gelu_tanh(h).astype(BF16), w2_ref[...]).astype(BF16)

    one(kc_ref, pek_ref, wka_ref, wkb_ref, bk_ref, wk2_ref, kout_ref)
    one(vc_ref, pev_ref, wva_ref, wvb_ref, bv_ref, wv2_ref, vout_ref)


def _nsa_kernel(q_ref, kvsw_ref, kc_ref, vc_ref, gates_ref, et_ref, overlap_t_ref, wbias_ref, o_ref,
                qaug_scr, m_scr, l_scr, acc_scr, *, seq):
    tq = TQ_NSA
    rows = 8 * tq
    n_blk = seq // SEL_BLOCK
    i = pl.program_id(1)
    s0 = i * tq
    q = q_ref[...]
    lane_q = lax.broadcasted_iota(jnp.int32, (tq, LANES), 1)
    lo = lane_q < HALF
    zero = jnp.zeros((tq, LANES), BF16)
    for half in range(2):
        for c in range(4):
            r = 4 * half + c
            chunk = q[:, c * LANES:(c + 1) * LANES]
            qaug_scr[r * tq:(r + 1) * tq, 0:LANES] = jnp.where(lo if half == 0 else jnp.logical_not(lo), chunk, zero)
    qs = qaug_scr[:, 0:LANES]

    ncmp = kc_ref.shape[0]
    sc = _dot_nt(qs, kc_ref[...])
    n_idx = lax.broadcasted_iota(jnp.int32, (rows, ncmp), 1)
    pos_c = s0 + (lax.broadcasted_iota(jnp.int32, (rows, ncmp), 0) & (tq - 1))
    cmask = n_idx * CMP_STRIDE + (CMP_BLOCK - 1) <= pos_c
    sm = jnp.where(cmask, sc, NEG)
    e = jnp.where(cmask, jnp.exp2(sm - jnp.max(sm, axis=-1, keepdims=True)), 0.0)
    p_cmp = e / jnp.maximum(jnp.sum(e, axis=-1, keepdims=True), 1e-30)
    p_blocks = [p_cmp[r * tq:(r + 1) * tq] for r in range(8)]
    o_cmp = _dot(p_cmp.astype(BF16), vc_ref[...])

    blk = lax.broadcasted_iota(jnp.int32, (n_blk, tq), 0)
    cur = (s0 + lax.broadcasted_iota(jnp.int32, (n_blk, tq), 1)) // SEL_BLOCK
    valid = blk <= cur
    forced = (blk == 0) | (blk == cur) | (blk == cur - 1)
    pad = jnp.zeros((LANES - n_blk, tq), F32)
    for g in range(2):
        psum = (p_blocks[4 * g] + p_blocks[4 * g + 1]) + (p_blocks[4 * g + 2] + p_blocks[4 * g + 3])
        hi = psum.astype(BF16)
        lo_part = (psum - hi.astype(F32)).astype(BF16)
        imp = _dot_nt(overlap_t_ref[...], hi) + _dot_nt(overlap_t_ref[...], lo_part)
        impv = jnp.where(valid, jnp.where(forced, BIG, imp), -1.0)
        cnt = jnp.zeros((n_blk, tq), F32)
        for b in range(n_blk):
            other = impv[b:b + 1, :]
            beats = (other > impv) | ((other == impv) & (blk > b))
            cnt = cnt + jnp.where(beats, 1.0, 0.0)
        bias_t = jnp.where(valid & (cnt < float(SEL_TOPK)), 0.0, NEG)
        bias = jnp.concatenate([bias_t, pad], axis=0).T.astype(BF16)
        for r in range(4 * g, 4 * g + 4):
            qaug_scr[r * tq:(r + 1) * tq, LANES:2 * LANES] = bias

    m_scr[...] = jnp.full((rows, LANES), NEG, F32)
    l_scr[...] = jnp.zeros((rows, LANES), F32)
    acc_scr[...] = jnp.zeros((rows, LANES), F32)

    def sel_tile(kt, diagonal):
        k0 = pl.multiple_of(kt * TK_SEL, TK_SEL)
        k_aug = jnp.concatenate([kvsw_ref[pl.ds(k0, TK_SEL), 0:128], et_ref[pl.ds(k0, TK_SEL), :]], axis=1)
        v = kvsw_ref[pl.ds(k0, TK_SEL), 128:256]
        s = _dot_nt(qaug_scr[...], k_aug)
        if diagonal:
            kpos = k0 + lax.broadcasted_iota(jnp.int32, (rows, TK_SEL), 1)
            qpos = s0 + (lax.broadcasted_iota(jnp.int32, (rows, TK_SEL), 0) & (tq - 1))
            s = jnp.where(kpos <= qpos, s, NEG)
        m_prev = m_scr[...]
        m_new = jnp.maximum(m_prev, jnp.max(s, axis=-1, keepdims=True))
        alpha = jnp.exp2(m_prev - m_new)
        e = jnp.exp2(s - jnp.tile(m_new, (1, TK_SEL // LANES)))
        l_scr[...] = alpha * l_scr[...] + jnp.sum(e, axis=-1, keepdims=True)
        acc_scr[...] = alpha * acc_scr[...] + _dot(e.astype(BF16), v)
        m_scr[...] = m_new

    last = s0 // TK_SEL

    def full_tile(kt, carry):
        sel_tile(kt, False)
        return carry

    lax.fori_loop(0, last, full_tile, 0)
    sel_tile(last, True)
    o_sel = acc_scr[...] / l_scr[...]

    wk = WINDOW + tq
    w0 = pl.multiple_of(jnp.maximum(s0 - WINDOW, 0), tq)
    sw = _dot_nt(qs, kvsw_ref[pl.ds(w0, wk), 256:384])
    wbias = wbias_ref[jnp.minimum(i, WINDOW // tq)]
    p_win = _softmax_rows(jnp.concatenate([sw[r * tq:(r + 1) * tq] + wbias for r in range(8)], axis=0))
    o_win = _dot(p_win.astype(BF16), kvsw_ref[pl.ds(w0, wk), 384:512])

    gates = gates_ref[...]
    outs = []
    for r in range(8):
        sl = slice(r * tq, (r + 1) * tq)
        outs.append(gates[:, r:r + 1] * o_cmp[sl] + gates[:, 8 + r:9 + r] * o_sel[sl]
                    + gates[:, 16 + r:17 + r] * o_win[sl])
    for c in range(4):
        o_ref[:, c * LANES:(c + 1) * LANES] = jnp.where(lo, outs[c], outs[4 + c]).astype(BF16)


def _mla_kernel(q_ref, k_ref, v_ref, o_ref, m_scr, l_scr, acc_scr):
    tq = TQ_MLA
    i = pl.program_id(1)
    s0 = i * tq
    m_scr[...] = jnp.full(m_scr.shape, NEG, F32)
    l_scr[...] = jnp.zeros(l_scr.shape, F32)
    acc_scr[...] = jnp.zeros(acc_scr.shape, F32)

    def tile(kt, diagonal):
        k0 = pl.multiple_of(kt * TK_MLA, TK_MLA)
        if diagonal:
            kpos = k0 + lax.broadcasted_iota(jnp.int32, (tq, TK_MLA), 1)
            causal = kpos <= s0 + lax.broadcasted_iota(jnp.int32, (tq, TK_MLA), 0)
        scores = [_dot_nt(q_ref[:, h * LANES:(h + 1) * LANES], k_ref[pl.ds(k0, TK_MLA), h * LANES:(h + 1) * LANES])
                  for h in range(MLA_HEADS)]
        for h in range(MLA_HEADS):
            vh = v_ref[pl.ds(k0, TK_MLA), (h // 2) * LANES:(h // 2 + 1) * LANES]
            s = jnp.where(causal, scores[h], NEG) if diagonal else scores[h]
            m_prev = m_scr[h]
            m_new = jnp.maximum(m_prev, jnp.max(s, axis=-1, keepdims=True))
            alpha = jnp.exp2(m_prev - m_new)
            e = jnp.exp2(s - jnp.tile(m_new, (1, TK_MLA // LANES)))
            l_scr[h] = alpha * l_scr[h] + jnp.sum(e, axis=-1, keepdims=True)
            acc_scr[h] = alpha * acc_scr[h] + _dot(e.astype(BF16), vh)
            m_scr[h] = m_new

    last = s0 // TK_MLA

    def full_tile(kt, carry):
        tile(kt, False)
        return carry

    lax.fori_loop(0, last, full_tile, 0)
    tile(last, True)
    lo = lax.broadcasted_iota(jnp.int32, (tq, LANES), 1) < HALF
    for c in range(4):
        oa = acc_scr[2 * c] / l_scr[2 * c]
        ob = acc_scr[2 * c + 1] / l_scr[2 * c + 1]
        o_ref[:, c * LANES:(c + 1) * LANES] = jnp.where(lo, oa, ob).astype(BF16)


def _merge_kernel(x_ref, onsa_ref, omla_ref, wa_ref, wb_ref, wmg_ref, wout_ref, g_ref, b_ref, y_ref):
    x = x_ref[...]
    ya = _dot(onsa_ref[...], wa_ref[...])
    yb = _dot(omla_ref[...], wb_ref[...])
    mg = _sigmoid(_dot(x.astype(BF16), wmg_ref[...]))
    mixed = (mg[:, :D_MODEL] * ya + mg[:, D_MODEL:] * yb).astype(BF16)
    y_ref[...] = _layer_norm(DEEPNORM_ALPHA * x + _dot(mixed, wout_ref[...]), g_ref[...], b_ref[...])


def _ffn_kernel(x_ref, wg_ref, wu_ref, wd_ref, cw_ref, cb_ref, g_ref, b_ref, y_ref, tail_scr, acc_scr):
    tm = TM_FFN
    first_tile = pl.program_id(1) == 0
    x = x_ref[...]
    xb = x.astype(BF16)
    row = lax.broadcasted_iota(jnp.int32, (tm, FC_FFN), 0)
    acc_scr[...] = jnp.zeros(acc_scr.shape, F32)

    @pl.when(first_tile)
    def _():
        tail_scr[...] = jnp.zeros(tail_scr.shape, F32)

    def chunk(fc, carry):
        a = _dot(xb, wg_ref[fc])
        tail = tail_scr[fc]
        tail_scr[fc] = a[tm - 8:, :]
        a1 = jnp.where(row == 0, tail[7:8, :], pltpu.roll(a, 1, 0))
        a2 = jnp.where(row == 0, tail[6:7, :], jnp.where(row == 1, tail[7:8, :], pltpu.roll(a, 2, 0)))
        cw = cw_ref[fc]
        conv = cw[0:1, :] * a2 + cw[1:2, :] * a1 + cw[2:3, :] * a + cb_ref[fc][0:1, :]
        hidden = (_gelu_tanh(conv) * _dot(xb, wu_ref[fc])).astype(BF16)
        acc_scr[...] += _dot(hidden, wd_ref[fc])
        return carry

    lax.fori_loop(0, D_FF // FC_FFN, chunk, 0)
    y_ref[...] = _layer_norm(DEEPNORM_ALPHA * x + acc_scr[...], g_ref[...], b_ref[...])


def _rope_tables(seq):
    pos = jnp.arange(seq, dtype=F32)

    def cs(rot):
        inv = ROPE_THETA ** (-jnp.arange(0, rot, 2, dtype=F32) / rot)
        ang = pos[:, None] * inv[None, :]
        return jnp.cos(ang), jnp.sin(ang)

    one = lambda n: jnp.ones((seq, n), F32)
    zero = lambda n: jnp.zeros((seq, n), F32)
    c8, s8 = cs(NSA_ROT_DIM)
    c16, s16 = cs(MLA_ROPE_DIM)
    nsa_c = jnp.concatenate([c8, c8, one(48)] * 2, axis=1)
    nsa_s = jnp.concatenate([-s8, s8, zero(48)] * 2, axis=1)
    mq_c = jnp.concatenate([one(64), c16, c16, one(32)], axis=1)
    mq_s = jnp.concatenate([zero(64), -s16, s16, zero(32)], axis=1)
    mk_c = jnp.concatenate([c16, c16, one(96)], axis=1)
    mk_s = jnp.concatenate([-s16, s16, zero(96)], axis=1)
    return jnp.stack([nsa_c, nsa_s, mq_c, mq_s, mk_c, mk_s])


def _head_perm():
    return np.array([(c + 4 * half) * 64 + d for c in range(4) for half in range(2) for d in range(64)])


def _window_bias(tq):
    wk = WINDOW + tq
    out = np.zeros((WINDOW // tq + 1, tq, wk), np.float32)
    for v in range(WINDOW // tq + 1):
        s0 = v * tq
        w0 = max(s0 - WINDOW, 0)
        pos = s0 + np.arange(tq)[:, None]
        kpos = w0 + np.arange(wk)[None, :]
        out[v] = np.where((kpos <= pos) & (pos - kpos < WINDOW), 0.0, NEG)
    return jnp.asarray(out)


def _const(shape):
    nd = len(shape)
    return pl.BlockSpec(shape, lambda *_: (0,) * nd)


def _params(n_axes):
    return pltpu.CompilerParams(dimension_semantics=("arbitrary",) * n_axes, vmem_limit_bytes=VMEM_LIMIT)


def kernel(x, w_in, cmp_pe_k, cmp_pe_v, cmp_k_w1, cmp_k_b1, cmp_k_w2, cmp_v_w1, cmp_v_b1, cmp_v_w2,
           nsa_w_o, mla_q_norm, mla_w_uq, mla_kv_norm, mla_w_uk, mla_w_uv, mla_w_o, w_out,
           ln1_g, ln1_b, ffn_w_gate, ffn_w_up, ffn_conv_w, ffn_conv_b, ffn_w_down, ln2_g, ln2_b):
    b, s, d = x.shape
    assert d == D_MODEL and s % TK_SEL == 0 and s % TM_FFN == 0 and (s // CMP_STRIDE) % 8 == 0
    assert WINDOW % TQ_NSA == 0 and s >= WINDOW + TQ_NSA
    t = b * s
    perm = _head_perm()
    xf = x.reshape(t, d)
    for l in range(DEPTH):
        xf = _layer(xf, b, s, perm, w_in[l], cmp_pe_k[l], cmp_pe_v[l], cmp_k_w1[l], cmp_k_b1[l], cmp_k_w2[l],
                    cmp_v_w1[l], cmp_v_b1[l], cmp_v_w2[l], nsa_w_o[l], mla_q_norm[l], mla_w_uq[l],
                    mla_kv_norm[l], mla_w_uk[l], mla_w_uv[l], mla_w_o[l], w_out[l], ln1_g[l], ln1_b[l],
                    ffn_w_gate[l], ffn_w_up[l], ffn_conv_w[l], ffn_conv_b[l], ffn_w_down[l], ln2_g[l], ln2_b[l])
    return xf.reshape(b, s, d)


def _layer(xf, b, s, perm, w_in, pe_k, pe_v, ck_w1, ck_b1, ck_w2, cv_w1, cv_b1, cv_w2, nsa_w_o, q_norm, w_uq,
           kv_norm, w_uk, w_uv, mla_w_o, w_out, ln1_g, ln1_b, w_gate, w_up, conv_w, conv_b, w_down, ln2_g, ln2_b):
    t = b * s
    d = D_MODEL
    o = np.cumsum([0, 512, 128, 128, 128, 128, 128, 128, 24, MLA_Q_RANK, MLA_KV_RANK, MLA_ROPE_DIM, 2 * d])
    wq = w_in[:, o[0]:o[1]][:, perm].astype(BF16)
    wkv = w_in[:, o[1]:o[7]].astype(BF16)
    gate_cols = np.array([h * 3 + br for br in range(3) for h in range(8)])
    wg = jnp.pad(w_in[:, o[7]:o[8]][:, gate_cols], ((0, 0), (0, LANES - 24))).astype(BF16)
    wcq = w_in[:, o[8]:o[9]].astype(BF16)
    wckv = w_in[:, o[9]:o[10]].astype(BF16)
    wkr = jnp.pad(w_in[:, o[10]:o[11]], ((0, 0), (0, LANES - MLA_ROPE_DIM))).astype(BF16)
    wmg = w_in[:, o[11]:o[12]].astype(BF16)

    wuq = jnp.pad(w_uq.reshape(MLA_Q_RANK, MLA_HEADS, MLA_QK_DIM), ((0, 0), (0, 0), (0, LANES - MLA_QK_DIM)))
    wuq = wuq.reshape(MLA_Q_RANK, MLA_HEADS * LANES).astype(BF16)
    wuk = jnp.pad(w_uk.reshape(MLA_KV_RANK, MLA_HEADS, MLA_NOPE_DIM), ((0, 0), (0, 0), (0, LANES - MLA_NOPE_DIM)))
    wuk = wuk.reshape(MLA_KV_RANK, MLA_HEADS * LANES).astype(BF16)
    place = np.zeros((LANES, MLA_HEADS * LANES), np.float32)
    for h in range(MLA_HEADS):
        for r in range(MLA_ROPE_DIM):
            place[r, h * LANES + MLA_NOPE_DIM + r] = 1.0
    place = jnp.asarray(place, BF16)
    tabs = _rope_tables(s)

    tm = TM_PROJ
    n_s = s // tm
    row_blk = lambda w: pl.BlockSpec((tm, w), lambda i: (i, 0))
    outs = pl.pallas_call(
        _proj_kernel,
        grid=(t // tm,),
        in_specs=[row_blk(d), pl.BlockSpec((6, tm, LANES), lambda i: (0, i % n_s, 0)),
                  _const(wq.shape), _const(wkv.shape), _const(wg.shape), _const(wcq.shape), _const(wckv.shape),
                  _const(wkr.shape), _const((1, MLA_Q_RANK)), _const((1, MLA_KV_RANK)), _const(wuq.shape),
                  _const(wuk.shape), _const((MLA_KV_RANK, 512)), _const(place.shape)],
        out_specs=[row_blk(512), row_blk(128), row_blk(128), row_blk(512), row_blk(128),
                   row_blk(1024), row_blk(1024), row_blk(512)],
        out_shape=[jax.ShapeDtypeStruct((t, 512), BF16), jax.ShapeDtypeStruct((t, 128), F32),
                   jax.ShapeDtypeStruct((t, 128), F32), jax.ShapeDtypeStruct((t, 512), BF16),
                   jax.ShapeDtypeStruct((t, 128), F32), jax.ShapeDtypeStruct((t, 1024), BF16),
                   jax.ShapeDtypeStruct((t, 1024), BF16), jax.ShapeDtypeStruct((t, 512), BF16)],
        compiler_params=_params(1),
        name="proj",
    )(xf, tabs, wq, wkv, wg, wcq, wckv, wkr, q_norm.reshape(1, -1), kv_norm.reshape(1, -1), wuq, wuk,
      w_uv.astype(BF16), place)
    qnsa, kcmp, vcmp, kvsw, gates, qmla, kmla, vmla = outs

    nch = s // CMP_STRIDE
    flat = CMP_STRIDE * LANES

    def big_w1(w1):
        w = w1.reshape(2, CMP_STRIDE, NSA_HEAD_DIM, CMP_HIDDEN)
        eye = jnp.eye(2, dtype=F32)
        return jnp.einsum('hldj,ge->hlgdej', w, eye).reshape(2, flat, 2 * CMP_HIDDEN).astype(BF16)

    def big_pe(pe):
        return jnp.tile(pe.reshape(2, CMP_STRIDE, 1, NSA_HEAD_DIM), (1, 1, 2, 1)).reshape(2, flat)

    def big_w2(w2):
        return jnp.einsum('jd,ge->gjed', w2, jnp.eye(2, dtype=F32)).reshape(2 * CMP_HIDDEN, LANES).astype(BF16)

    wk1, wv1 = big_w1(ck_w1), big_w1(cv_w1)
    bk = jnp.tile(ck_b1, 2).reshape(1, -1)
    bv = jnp.tile(cv_b1, 2).reshape(1, -1)
    tok_blk = pl.BlockSpec((None, nch, flat), lambda i: (i, 0, 0))
    cmp_blk = pl.BlockSpec((None, nch, LANES), lambda i: (i, 0, 0))
    kc, vc = pl.pallas_call(
        _compress_kernel,
        grid=(b,),
        in_specs=[tok_blk, tok_blk, _const((2, flat)), _const((2, flat)),
                  _const((flat, 512)), _const((flat, 512)), _const((flat, 512)), _const((flat, 512)),
                  _const((1, 512)), _const((1, 512)), _const((512, LANES)), _const((512, LANES))],
        out_specs=[cmp_blk, cmp_blk],
        out_shape=[jax.ShapeDtypeStruct((b, nch, LANES), BF16)] * 2,
        compiler_params=_params(1),
        name="compress",
    )(kcmp.reshape(b, nch, flat), vcmp.reshape(b, nch, flat), big_pe(pe_k), big_pe(pe_v),
      wk1[0], wk1[1], wv1[0], wv1[1], bk, bv, big_w2(ck_w2), big_w2(cv_w2))

    n_sel = s // SEL_BLOCK
    key_blk = np.arange(s) // SEL_BLOCK
    onehot_t = jnp.asarray((key_blk[:, None] == np.arange(LANES)[None, :]).astype(np.float32), BF16)
    n_i = np.arange(nch)[None, :]
    j_i = np.arange(n_sel)[:, None]
    overlap_t = ((n_i <= 4 * j_i + 3) & (n_i >= 4 * j_i - 1) & (n_i < nch - 1)).astype(np.float32)
    overlap_t = jnp.asarray(overlap_t, BF16)
    tq = TQ_NSA
    nq = s // tq
    wbias = _window_bias(tq)
    onsa = pl.pallas_call(
        functools.partial(_nsa_kernel, seq=s),
        grid=(b, nq),
        in_specs=[pl.BlockSpec((tq, 512), lambda bi, i: (bi * nq + i, 0)),
                  pl.BlockSpec((s, 512), lambda bi, i: (bi, 0)),
                  pl.BlockSpec((None, nch, LANES), lambda bi, i: (bi, 0, 0)),
                  pl.BlockSpec((None, nch, LANES), lambda bi, i: (bi, 0, 0)),
                  pl.BlockSpec((tq, LANES), lambda bi, i: (bi * nq + i, 0)),
                  _const(onehot_t.shape), _const(overlap_t.shape), _const(wbias.shape)],
        out_specs=pl.BlockSpec((tq, 512), lambda bi, i: (bi * nq + i, 0)),
        out_shape=jax.ShapeDtypeStruct((t, 512), BF16),
        scratch_shapes=[pltpu.VMEM((8 * tq, 2 * LANES), BF16), pltpu.VMEM((8 * tq, LANES), F32),
                        pltpu.VMEM((8 * tq, LANES), F32), pltpu.VMEM((8 * tq, LANES), F32)],
        compiler_params=_params(2),
        name="nsa",
    )(qnsa, kvsw, kc, vc, gates, onehot_t, overlap_t, wbias)

    tq = TQ_MLA
    nq = s // tq
    omla = pl.pallas_call(
        _mla_kernel,
        grid=(b, nq),
        in_specs=[pl.BlockSpec((tq, 1024), lambda bi, i: (bi * nq + i, 0)),
                  pl.BlockSpec((s, 1024), lambda bi, i: (bi, 0)),
                  pl.BlockSpec((s, 512), lambda bi, i: (bi, 0))],
        out_specs=pl.BlockSpec((tq, 512), lambda bi, i: (bi * nq + i, 0)),
        out_shape=jax.ShapeDtypeStruct((t, 512), BF16),
        scratch_shapes=[pltpu.VMEM((MLA_HEADS, tq, LANES), F32), pltpu.VMEM((MLA_HEADS, tq, LANES), F32),
                        pltpu.VMEM((MLA_HEADS, tq, LANES), F32)],
        compiler_params=_params(2),
        name="mla",
    )(qmla, kmla, vmla)

    tm = TM_MERGE
    row_blk = lambda w: pl.BlockSpec((tm, w), lambda i: (i, 0))
    x1 = pl.pallas_call(
        _merge_kernel,
        grid=(t // tm,),
        in_specs=[row_blk(d), row_blk(512), row_blk(512), _const((512, d)), _const((512, d)),
                  _const((d, 2 * d)), _const((d, d)), _const((1, d)), _const((1, d))],
        out_specs=row_blk(d),
        out_shape=jax.ShapeDtypeStruct((t, d), F32),
        compiler_params=_params(1),
        name="merge",
    )(xf, onsa, omla, nsa_w_o[perm, :].astype(BF16), mla_w_o.astype(BF16), wmg, w_out.astype(BF16),
      ln1_g.reshape(1, -1), ln1_b.reshape(1, -1))

    tm = TM_FFN
    nfc = D_FF // FC_FFN
    n_s = s // tm
    wg3 = w_gate.reshape(d, nfc, FC_FFN).transpose(1, 0, 2).astype(BF16)
    wu3 = w_up.reshape(d, nfc, FC_FFN).transpose(1, 0, 2).astype(BF16)
    wd3 = w_down.reshape(nfc, FC_FFN, d).astype(BF16)
    cw3 = jnp.pad(conv_w, ((0, 5), (0, 0))).reshape(8, nfc, FC_FFN).transpose(1, 0, 2)
    cb3 = jnp.broadcast_to(conv_b.reshape(nfc, 1, FC_FFN), (nfc, 8, FC_FFN))
    blk = pl.BlockSpec((tm, d), lambda bi, i: (bi * n_s + i, 0))
    y = pl.pallas_call(
        _ffn_kernel,
        grid=(b, n_s),
        in_specs=[blk, _const(wg3.shape), _const(wu3.shape), _const(wd3.shape), _const(cw3.shape),
                  _const(cb3.shape), _const((1, d)), _const((1, d))],
        out_specs=blk,
        out_shape=jax.ShapeDtypeStruct((t, d), F32),
        scratch_shapes=[pltpu.VMEM((nfc, 8, FC_FFN), F32), pltpu.VMEM((tm, d), F32)],
        compiler_params=_params(2),
        name="ffn",
    )(x1, wg3, wu3, wd3, cw3, cb3, ln2_g.reshape(1, -1), ln2_b.reshape(1, -1))
    return y
```

```python
import functools

import numpy as np
import jax
import jax.numpy as jnp
from jax import lax
from jax.experimental import pallas as pl
from jax.experimental.pallas import tpu as pltpu

D_MODEL = 1024
ROPE_THETA = 500000.0
NSA_HEADS = 8
NSA_KV_GROUPS = 2
NSA_GROUP = 4
NSA_HEAD_DIM = 64
NSA_ROT_DIM = 16
CMP_BLOCK = 32
CMP_STRIDE = 16
CMP_HIDDEN = 256
SEL_BLOCK = 64
SEL_TOPK = 8
WINDOW = 256
MLA_HEADS = 8
MLA_Q_RANK = 768
MLA_KV_RANK = 256
MLA_NOPE_DIM = 64
MLA_ROPE_DIM = 32
MLA_V_DIM = 64
MLA_QK_DIM = 96
D_FF = 2816
LN_EPS = 1e-5
RMS_EPS = 1e-6
DEPTH = 1
DEEPNORM_ALPHA = (2 * DEPTH) ** 0.25
LOG2E = 1.4426950408889634

LANES = 128
HALF = LANES // 2
NEG = -1e30
BIG = 3e38
VMEM_LIMIT = 56 * 1024 * 1024

TM_PROJ = 512
TQ_NSA = 128
TK_SEL = 512
TQ_MLA = 512
TK_MLA = 512
TM_MERGE = 512
TM_FFN = 512
FC_FFN = 256

BF16 = jnp.bfloat16
F32 = jnp.float32


def _dot(a, b):
    return jnp.dot(a, b, preferred_element_type=F32)


def _dot_nt(a, b):
    return lax.dot_general(a, b, (((1,), (1,)), ((), ())), preferred_element_type=F32)


def _gelu_tanh(x):
    return x * (0.5 * (1.0 + jnp.tanh(0.7978845608028654 * (x + 0.044715 * (x * x * x)))))


def _sigmoid(x):
    return 1.0 / (1.0 + jnp.exp(-x))


def _layer_norm(x, g, b):
    mu = jnp.mean(x, axis=-1, keepdims=True)
    xc = x - mu
    var = jnp.mean(xc * xc, axis=-1, keepdims=True)
    return xc * lax.rsqrt(var + LN_EPS) * g + b


def _rms_norm(x, g):
    return x * lax.rsqrt(jnp.mean(x * x, axis=-1, keepdims=True) + RMS_EPS) * g


def _rope_chunk(x, cos, sin_signed, first_half, half):
    partner = jnp.where(first_half, pltpu.roll(x, LANES - half, 1), pltpu.roll(x, half, 1))
    return x * cos + partner * sin_signed


def _softmax_rows(s):
    e = jnp.exp2(s - jnp.max(s, axis=-1, keepdims=True))
    return e / jnp.sum(e, axis=-1, keepdims=True)


def _proj_kernel(x_ref, tab_ref, wq_ref, wkv_ref, wg_ref, wcq_ref, wckv_ref, wkr_ref,
                 qn_ref, kvn_ref, wuq_ref, wuk_ref, wuv_ref, place_ref,
                 qnsa_ref, kcmp_ref, vcmp_ref, kvsw_ref, gates_ref, qmla_ref, kmla_ref, vmla_ref):
    xb = x_ref[...].astype(BF16)
    tm = xb.shape[0]
    lane = lax.broadcasted_iota(jnp.int32, (tm, LANES), 1)
    nsa_first = (lane % HALF) < (NSA_ROT_DIM // 2)
    mlaq_first = lane < (MLA_NOPE_DIM + MLA_ROPE_DIM // 2)
    mlak_first = lane < (MLA_ROPE_DIM // 2)
    cn, sn = tab_ref[0], tab_ref[1]
    cq_t, sq_t = tab_ref[2], tab_ref[3]
    ck_t, sk_t = tab_ref[4], tab_ref[5]

    def rope_n(v):
        return _rope_chunk(v, cn, sn, nsa_first, NSA_ROT_DIM // 2)

    q = _dot(xb, wq_ref[...])
    kv = _dot(xb, wkv_ref[...])
    cq_raw = _dot(xb, wcq_ref[...])
    ckv_raw = _dot(xb, wckv_ref[...])
    kr_raw = _dot(xb, wkr_ref[...])
    gate_logits = _dot(xb, wg_ref[...])

    cq = _rms_norm(cq_raw, qn_ref[...]).astype(BF16)
    qm = _dot(cq, wuq_ref[...])
    ckv = _rms_norm(ckv_raw, kvn_ref[...]).astype(BF16)
    kr = _rope_chunk(kr_raw, ck_t, sk_t, mlak_first, MLA_ROPE_DIM // 2).astype(BF16)
    kmla_ref[...] = (_dot(ckv, wuk_ref[...]) + _dot(kr, place_ref[...])).astype(BF16)
    vmla_ref[...] = _dot(ckv, wuv_ref[...]).astype(BF16)

    for c in range(4):
        sl = slice(c * LANES, (c + 1) * LANES)
        qnsa_ref[:, sl] = (rope_n(q[:, sl]) * (NSA_HEAD_DIM ** -0.5 * LOG2E)).astype(BF16)

    kcmp_ref[...] = rope_n(kv[:, 0:128])
    vcmp_ref[...] = kv[:, 128:256]
    kvsw_ref[:, 0:128] = rope_n(kv[:, 256:384]).astype(BF16)
    kvsw_ref[:, 128:256] = kv[:, 384:512].astype(BF16)
    kvsw_ref[:, 256:384] = rope_n(kv[:, 512:640]).astype(BF16)
    kvsw_ref[:, 384:512] = kv[:, 640:768].astype(BF16)

    gates_ref[...] = _sigmoid(gate_logits)

    for h in range(MLA_HEADS):
        sl = slice(h * LANES, (h + 1) * LANES)
        roped = _rope_chunk(qm[:, sl], cq_t, sq_t, mlaq_first, MLA_ROPE_DIM // 2)
        qmla_ref[:, sl] = (roped * (MLA_QK_DIM ** -0.5 * LOG2E)).astype(BF16)


def _compress_kernel(kc_ref, vc_ref, pek_ref, pev_ref, wka_ref, wkb_ref, wva_ref, wvb_ref,
                     bk_ref, bv_ref, wk2_ref, wv2_ref, kout_ref, vout_ref):
    def one(tok_ref, pe_ref, wa_ref, wb_ref, b_ref, w2_ref, out_ref):
        ch = tok_ref[...]
        nxt = pltpu.roll(ch, ch.shape[0] - 1, 0)
        a = (ch + pe_ref[0:1, :]).astype(BF16)
        b = (nxt + pe_ref[1:2, :]).astype(BF16)
        h = _dot(a, wa_ref[...]) + _dot(b, wb_ref[...]) + b_ref[...]
        out_ref[...] = _dot(_gelu_tanh(h).astype(BF16), w2_ref[...]).astype(BF16)

    one(kc_ref, pek_ref, wka_ref, wkb_ref, bk_ref, wk2_ref, kout_ref)
    one(vc_ref, pev_ref, wva_ref, wvb_ref, bv_ref, wv2_ref, vout_ref)


def _nsa_kernel(q_ref, kvsw_ref, kc_ref, vc_ref, gates_ref, et_ref, overlap_t_ref, wbias_ref, cbias_ref, o_ref,
                qaug_scr, m_scr, l_scr, acc_scr, part_scr, *, seq):
    tq = TQ_NSA
    rows = 8 * tq
    n_blk = seq // SEL_BLOCK
    i = pl.program_id(1)
    s0 = i * tq
    q = q_ref[...]
    lane_q = lax.broadcasted_iota(jnp.int32, (tq, LANES), 1)
    lo = lane_q < HALF
    zero = jnp.zeros((tq, LANES), BF16)
    for half in range(2):
        for c in range(4):
            r = 4 * half + c
            chunk = q[:, c * LANES:(c + 1) * LANES]
            qaug_scr[r * tq:(r + 1) * tq, 0:LANES] = jnp.where(lo if half == 0 else jnp.logical_not(lo), chunk, zero)
    qs = qaug_scr[:, 0:LANES]

    ncmp = kc_ref.shape[0]
    wk = WINDOW + tq
    w0 = pl.multiple_of(jnp.maximum(s0 - WINDOW, 0), tq)
    sc = _dot_nt(qs, kc_ref[...])
    sw = _dot_nt(qs, kvsw_ref[pl.ds(w0, wk), 256:384])
    n_idx = lax.broadcasted_iota(jnp.int32, (rows, ncmp), 1)
    pos_c = s0 + (lax.broadcasted_iota(jnp.int32, (rows, ncmp), 0) & (tq - 1))
    cmask = n_idx * CMP_STRIDE + (CMP_BLOCK - 1) <= pos_c
    sm = jnp.where(cmask, sc, NEG)
    e = jnp.where(cmask, jnp.exp2(sm - jnp.max(sm, axis=-1, keepdims=True)), 0.0)
    p_cmp = e / jnp.maximum(jnp.sum(e, axis=-1, keepdims=True), 1e-30)
    p_blocks = [p_cmp[r * tq:(r + 1) * tq] for r in range(8)]
    o_cmp = _dot(p_cmp.astype(BF16), vc_ref[...])
    wbias = wbias_ref[jnp.minimum(i, WINDOW // tq)]
    p_win = _softmax_rows(jnp.concatenate([sw[r * tq:(r + 1) * tq] + wbias for r in range(8)], axis=0))
    o_win = _dot(p_win.astype(BF16), kvsw_ref[pl.ds(w0, wk), 384:512])
    gates = gates_ref[...]
    for r in range(8):
        sl = slice(r * tq, (r + 1) * tq)
        part_scr[sl] = gates[:, r:r + 1] * o_cmp[sl] + gates[:, 16 + r:17 + r] * o_win[sl]

    blk = lax.broadcasted_iota(jnp.int32, (n_blk, tq), 0)
    cur = (s0 + lax.broadcasted_iota(jnp.int32, (n_blk, tq), 1)) // SEL_BLOCK
    valid = blk <= cur
    forced = (blk == 0) | (blk == cur) | (blk == cur - 1)
    pad = jnp.zeros((LANES - n_blk, tq), F32)
    for g in range(2):
        psum = (p_blocks[4 * g] + p_blocks[4 * g + 1]) + (p_blocks[4 * g + 2] + p_blocks[4 * g + 3])
        hi = psum.astype(BF16)
        lo_part = (psum - hi.astype(F32)).astype(BF16)
        imp = _dot_nt(overlap_t_ref[...], hi) + _dot_nt(overlap_t_ref[...], lo_part)
        impv = jnp.where(valid, jnp.where(forced, BIG, imp), -1.0)
        n_slab = n_blk // 8
        slabs = [impv[8 * v:8 * (v + 1)] for v in range(n_slab)]
        cnts = [jnp.zeros((8, tq), F32) for _ in range(n_slab)]
        for b in range(n_blk):
            other = impv[b:b + 1, :]
            for v in range(n_slab):
                if b < 8 * v:
                    beats = other >= slabs[v]
                elif b >= 8 * (v + 1):
                    beats = other > slabs[v]
                else:
                    beats = (other > slabs[v]) | ((other == slabs[v]) & (blk[0:8] > b - 8 * v))
                cnts[v] = cnts[v] + jnp.where(beats, 1.0, 0.0)
        cnt = jnp.concatenate(cnts, axis=0)
        bias_t = jnp.where(valid & (cnt < float(SEL_TOPK)), 0.0, NEG)
        bias = jnp.concatenate([bias_t, pad], axis=0).T.astype(BF16)
        for r in range(4 * g, 4 * g + 4):
            qaug_scr[r * tq:(r + 1) * tq, LANES:2 * LANES] = bias

    m_scr[...] = jnp.full((rows, LANES), NEG, F32)
    l_scr[...] = jnp.zeros((rows, LANES), F32)
    acc_scr[...] = jnp.zeros((rows, LANES), F32)

    def sel_tile(kt, diagonal):
        k0 = pl.multiple_of(kt * TK_SEL, TK_SEL)
        k_aug = jnp.concatenate([kvsw_ref[pl.ds(k0, TK_SEL), 0:128], et_ref[pl.ds(k0, TK_SEL), :]], axis=1)
        v = kvsw_ref[pl.ds(k0, TK_SEL), 128:256]
        s = _dot_nt(qaug_scr[...], k_aug)
        if diagonal:
            cb = cbias_ref[(s0 - k0) // tq]
            s = jnp.concatenate([s[r * tq:(r + 1) * tq] + cb for r in range(8)], axis=0)
        m_prev = m_scr[...]
        m_new = jnp.maximum(m_prev, jnp.max(s, axis=-1, keepdims=True))
        alpha = jnp.exp2(m_prev - m_new)
        e = jnp.exp2(s - jnp.tile(m_new, (1, TK_SEL // LANES)))
        l_scr[...] = alpha * l_scr[...] + jnp.sum(e, axis=-1, keepdims=True)
        acc_scr[...] = alpha * acc_scr[...] + _dot(e.astype(BF16), v)
        m_scr[...] = m_new

    last = s0 // TK_SEL

    def full_tile(kt, carry):
        sel_tile(kt, False)
        return carry

    lax.fori_loop(0, last, full_tile, 0)
    sel_tile(last, True)
    o_sel = acc_scr[...] / l_scr[...]

    gates = gates_ref[...]
    outs = []
    for r in range(8):
        sl = slice(r * tq, (r + 1) * tq)
        outs.append(part_scr[sl] + gates[:, 8 + r:9 + r] * o_sel[sl])
    for c in range(4):
        o_ref[:, c * LANES:(c + 1) * LANES] = jnp.where(lo, outs[c], outs[4 + c]).astype(BF16)


def _mla_kernel(q_ref, k_ref, v_ref, o_ref, m_scr, l_scr, acc_scr):
    tq = TQ_MLA
    i = pl.program_id(1)
    s0 = i * tq
    m_scr[...] = jnp.full(m_scr.shape, NEG, F32)
    l_scr[...] = jnp.zeros(l_scr.shape, F32)
    acc_scr[...] = jnp.zeros(acc_scr.shape, F32)

    def tile(kt, diagonal):
        k0 = pl.multiple_of(kt * TK_MLA, TK_MLA)
        if diagonal:
            kpos = k0 + lax.broadcasted_iota(jnp.int32, (tq, TK_MLA), 1)
            causal = kpos <= s0 + lax.broadcasted_iota(jnp.int32, (tq, TK_MLA), 0)
        scores = [_dot_nt(q_ref[:, h * LANES:(h + 1) * LANES], k_ref[pl.ds(k0, TK_MLA), h * LANES:(h + 1) * LANES])
                  for h in range(MLA_HEADS)]
        for h in range(MLA_HEADS):
            vh = v_ref[pl.ds(k0, TK_MLA), (h // 2) * LANES:(h // 2 + 1) * LANES]
            s = jnp.where(causal, scores[h], NEG) if diagonal else scores[h]
            m_prev = m_scr[h]
            m_new = jnp.maximum(m_prev, jnp.max(s, axis=-1, keepdims=True))
            alpha = jnp.exp2(m_prev - m_new)
            e = jnp.exp2(s - jnp.tile(m_new, (1, TK_MLA // LANES)))
            l_scr[h] = alpha * l_scr[h] + jnp.sum(e, axis=-1, keepdims=True)
            acc_scr[h] = alpha * acc_scr[h] + _dot(e.astype(BF16), vh)
            m_scr[h] = m_new

    last = s0 // TK_MLA

    def full_tile(kt, carry):
        tile(kt, False)
        return carry

    lax.fori_loop(0, last, full_tile, 0)
    tile(last, True)
    lo = lax.broadcasted_iota(jnp.int32, (tq, LANES), 1) < HALF
    for c in range(4):
        oa = acc_scr[2 * c] / l_scr[2 * c]
        ob = acc_scr[2 * c + 1] / l_scr[2 * c + 1]
        o_ref[:, c * LANES:(c + 1) * LANES] = jnp.where(lo, oa, ob).astype(BF16)


def _merge_kernel(x_ref, onsa_ref, omla_ref, wa_ref, wb_ref, wmg_ref, wout_ref, g_ref, b_ref, y_ref):
    n_sub = 2
    sub = x_ref.shape[0] // n_sub
    first = []
    for j in range(n_sub):
        rs = slice(j * sub, (j + 1) * sub)
        first.append((_dot(onsa_ref[rs, :], wa_ref[...]), _dot(omla_ref[rs, :], wb_ref[...]),
                      _dot(x_ref[rs, :].astype(BF16), wmg_ref[...])))
    for j in range(n_sub):
        rs = slice(j * sub, (j + 1) * sub)
        ya, yb, mg_logits = first[j]
        mg = _sigmoid(mg_logits)
        mixed = (mg[:, :D_MODEL] * ya + mg[:, D_MODEL:] * yb).astype(BF16)
        y_ref[rs, :] = _layer_norm(DEEPNORM_ALPHA * x_ref[rs, :] + _dot(mixed, wout_ref[...]), g_ref[...], b_ref[...])


def _ffn_kernel(x_ref, wg_ref, wu_ref, wd_ref, cw_ref, g_ref, b_ref, y_ref, tail_scr, h_scr):
    tm = TM_FFN
    nfc = D_FF // FC_FFN
    first_tile = pl.program_id(1) == 0
    x = x_ref[...]
    xb = x.astype(BF16)
    row = lax.broadcasted_iota(jnp.int32, (tm, FC_FFN), 0)

    @pl.when(first_tile)
    def _():
        tail_scr[...] = jnp.zeros(tail_scr.shape, F32)

    def gate_up(fc):
        cols = slice(fc * FC_FFN, (fc + 1) * FC_FFN)
        return _dot(xb, wg_ref[:, cols]), _dot(xb, wu_ref[:, cols])

    cur = gate_up(0)
    for fc in range(nfc):
        cols = slice(fc * FC_FFN, (fc + 1) * FC_FFN)
        nxt = gate_up(fc + 1) if fc + 1 < nfc else None
        a, up = cur
        tail = tail_scr[:, cols]
        tail_scr[:, cols] = a[tm - 8:, :]
        a1 = jnp.where(row == 0, tail[7:8, :], pltpu.roll(a, 1, 0))
        a2 = jnp.where(row == 0, tail[6:7, :], jnp.where(row == 1, tail[7:8, :], pltpu.roll(a, 2, 0)))
        cw = cw_ref[:, cols]
        conv = cw[0:1, :] * a2 + cw[1:2, :] * a1 + cw[2:3, :] * a + cw[3:4, :]
        h_scr[:, cols] = (_gelu_tanh(conv) * up).astype(BF16)
        cur = nxt
    y = _dot(h_scr[...], wd_ref[...])
    y_ref[...] = _layer_norm(DEEPNORM_ALPHA * x + y, g_ref[...], b_ref[...])


def _rope_tables(seq):
    pos = jnp.arange(seq, dtype=F32)

    def cs(rot):
        inv = ROPE_THETA ** (-jnp.arange(0, rot, 2, dtype=F32) / rot)
        ang = pos[:, None] * inv[None, :]
        return jnp.cos(ang), jnp.sin(ang)

    one = lambda n: jnp.ones((seq, n), F32)
    zero = lambda n: jnp.zeros((seq, n), F32)
    c8, s8 = cs(NSA_ROT_DIM)
    c16, s16 = cs(MLA_ROPE_DIM)
    nsa_c = jnp.concatenate([c8, c8, one(48)] * 2, axis=1)
    nsa_s = jnp.concatenate([-s8, s8, zero(48)] * 2, axis=1)
    mq_c = jnp.concatenate([one(64), c16, c16, one(32)], axis=1)
    mq_s = jnp.concatenate([zero(64), -s16, s16, zero(32)], axis=1)
    mk_c = jnp.concatenate([c16, c16, one(96)], axis=1)
    mk_s = jnp.concatenate([-s16, s16, zero(96)], axis=1)
    return jnp.stack([nsa_c, nsa_s, mq_c, mq_s, mk_c, mk_s])


def _head_perm():
    return np.array([(c + 4 * half) * 64 + d for c in range(4) for half in range(2) for d in range(64)])


def _window_bias(tq):
    wk = WINDOW + tq
    out = np.zeros((WINDOW // tq + 1, tq, wk), np.float32)
    for v in range(WINDOW // tq + 1):
        s0 = v * tq
        w0 = max(s0 - WINDOW, 0)
        pos = s0 + np.arange(tq)[:, None]
        kpos = w0 + np.arange(wk)[None, :]
        out[v] = np.where((kpos <= pos) & (pos - kpos < WINDOW), 0.0, NEG)
    return jnp.asarray(out)


def _const(shape):
    nd = len(shape)
    return pl.BlockSpec(shape, lambda *_: (0,) * nd, pipeline_mode=pl.Buffered(1))


def _params(n_axes):
    return pltpu.CompilerParams(dimension_semantics=("arbitrary",) * n_axes, vmem_limit_bytes=VMEM_LIMIT)


def kernel(x, w_in, cmp_pe_k, cmp_pe_v, cmp_k_w1, cmp_k_b1, cmp_k_w2, cmp_v_w1, cmp_v_b1, cmp_v_w2,
           nsa_w_o, mla_q_norm, mla_w_uq, mla_kv_norm, mla_w_uk, mla_w_uv, mla_w_o, w_out,
           ln1_g, ln1_b, ffn_w_gate, ffn_w_up, ffn_conv_w, ffn_conv_b, ffn_w_down, ln2_g, ln2_b):
    b, s, d = x.shape
    assert d == D_MODEL and s % TK_SEL == 0 and s % TM_FFN == 0 and (s // CMP_STRIDE) % 8 == 0
    assert WINDOW % TQ_NSA == 0 and s >= WINDOW + TQ_NSA
    t = b * s
    perm = _head_perm()
    xf = x.reshape(t, d)
    for l in range(DEPTH):
        xf = _layer(xf, b, s, perm, w_in[l], cmp_pe_k[l], cmp_pe_v[l], cmp_k_w1[l], cmp_k_b1[l], cmp_k_w2[l],
                    cmp_v_w1[l], cmp_v_b1[l], cmp_v_w2[l], nsa_w_o[l], mla_q_norm[l], mla_w_uq[l],
                    mla_kv_norm[l], mla_w_uk[l], mla_w_uv[l], mla_w_o[l], w_out[l], ln1_g[l], ln1_b[l],
                    ffn_w_gate[l], ffn_w_up[l], ffn_conv_w[l], ffn_conv_b[l], ffn_w_down[l], ln2_g[l], ln2_b[l])
    return xf.reshape(b, s, d)


def _layer(xf, b, s, perm, w_in, pe_k, pe_v, ck_w1, ck_b1, ck_w2, cv_w1, cv_b1, cv_w2, nsa_w_o, q_norm, w_uq,
           kv_norm, w_uk, w_uv, mla_w_o, w_out, ln1_g, ln1_b, w_gate, w_up, conv_w, conv_b, w_down, ln2_g, ln2_b):
    t = b * s
    d = D_MODEL
    o = np.cumsum([0, 512, 128, 128, 128, 128, 128, 128, 24, MLA_Q_RANK, MLA_KV_RANK, MLA_ROPE_DIM, 2 * d])
    wq = w_in[:, o[0]:o[1]][:, perm].astype(BF16)
    wkv = w_in[:, o[1]:o[7]].astype(BF16)
    gate_cols = np.array([h * 3 + br for br in range(3) for h in range(8)])
    wg = jnp.pad(w_in[:, o[7]:o[8]][:, gate_cols], ((0, 0), (0, LANES - 24))).astype(BF16)
    wcq = w_in[:, o[8]:o[9]].astype(BF16)
    wckv = w_in[:, o[9]:o[10]].astype(BF16)
    wkr = jnp.pad(w_in[:, o[10]:o[11]], ((0, 0), (0, LANES - MLA_ROPE_DIM))).astype(BF16)
    wmg = w_in[:, o[11]:o[12]].astype(BF16)

    wuq = jnp.pad(w_uq.reshape(MLA_Q_RANK, MLA_HEADS, MLA_QK_DIM), ((0, 0), (0, 0), (0, LANES - MLA_QK_DIM)))
    wuq = wuq.reshape(MLA_Q_RANK, MLA_HEADS * LANES).astype(BF16)
    wuk = jnp.pad(w_uk.reshape(MLA_KV_RANK, MLA_HEADS, MLA_NOPE_DIM), ((0, 0), (0, 0), (0, LANES - MLA_NOPE_DIM)))
    wuk = wuk.reshape(MLA_KV_RANK, MLA_HEADS * LANES).astype(BF16)
    place = np.zeros((LANES, MLA_HEADS * LANES), np.float32)
    for h in range(MLA_HEADS):
        for r in range(MLA_ROPE_DIM):
            place[r, h * LANES + MLA_NOPE_DIM + r] = 1.0
    place = jnp.asarray(place, BF16)
    tabs = _rope_tables(s)

    tm = TM_PROJ
    n_s = s // tm
    row_blk = lambda w: pl.BlockSpec((tm, w), lambda i: (i, 0))
    outs = pl.pallas_call(
        _proj_kernel,
        grid=(t // tm,),
        in_specs=[row_blk(d), pl.BlockSpec((6, tm, LANES), lambda i: (0, i % n_s, 0)),
                  _const(wq.shape), _const(wkv.shape), _const(wg.shape), _const(wcq.shape), _const(wckv.shape),
                  _const(wkr.shape), _const((1, MLA_Q_RANK)), _const((1, MLA_KV_RANK)), _const(wuq.shape),
                  _const(wuk.shape), _const((MLA_KV_RANK, 512)), _const(place.shape)],
        out_specs=[row_blk(512), row_blk(128), row_blk(128), row_blk(512), row_blk(128),
                   row_blk(1024), row_blk(1024), row_blk(512)],
        out_shape=[jax.ShapeDtypeStruct((t, 512), BF16), jax.ShapeDtypeStruct((t, 128), F32),
                   jax.ShapeDtypeStruct((t, 128), F32), jax.ShapeDtypeStruct((t, 512), BF16),
                   jax.ShapeDtypeStruct((t, 128), F32), jax.ShapeDtypeStruct((t, 1024), BF16),
                   jax.ShapeDtypeStruct((t, 1024), BF16), jax.ShapeDtypeStruct((t, 512), BF16)],
        compiler_params=_params(1),
        name="proj",
    )(xf, tabs, wq, wkv, wg, wcq, wckv, wkr, q_norm.reshape(1, -1), kv_norm.reshape(1, -1), wuq, wuk,
      w_uv.astype(BF16), place)
    qnsa, kcmp, vcmp, kvsw, gates, qmla, kmla, vmla = outs

    nch = s // CMP_STRIDE
    flat = CMP_STRIDE * LANES

    def big_w1(w1):
        w = w1.reshape(2, CMP_STRIDE, NSA_HEAD_DIM, CMP_HIDDEN)
        eye = jnp.eye(2, dtype=F32)
        return jnp.einsum('hldj,ge->hlgdej', w, eye).reshape(2, flat, 2 * CMP_HIDDEN).astype(BF16)

    def big_pe(pe):
        return jnp.tile(pe.reshape(2, CMP_STRIDE, 1, NSA_HEAD_DIM), (1, 1, 2, 1)).reshape(2, flat)

    def big_w2(w2):
        return jnp.einsum('jd,ge->gjed', w2, jnp.eye(2, dtype=F32)).reshape(2 * CMP_HIDDEN, LANES).astype(BF16)

    wk1, wv1 = big_w1(ck_w1), big_w1(cv_w1)
    bk = jnp.tile(ck_b1, 2).reshape(1, -1)
    bv = jnp.tile(cv_b1, 2).reshape(1, -1)
    tok_blk = pl.BlockSpec((None, nch, flat), lambda i: (i, 0, 0))
    cmp_blk = pl.BlockSpec((None, nch, LANES), lambda i: (i, 0, 0))
    kc, vc = pl.pallas_call(
        _compress_kernel,
        grid=(b,),
        in_specs=[tok_blk, tok_blk, _const((2, flat)), _const((2, flat)),
                  _const((flat, 512)), _const((flat, 512)), _const((flat, 512)), _const((flat, 512)),
                  _const((1, 512)), _const((1, 512)), _const((512, LANES)), _const((512, LANES))],
        out_specs=[cmp_blk, cmp_blk],
        out_shape=[jax.ShapeDtypeStruct((b, nch, LANES), BF16)] * 2,
        compiler_params=_params(1),
        name="compress",
    )(kcmp.reshape(b, nch, flat), vcmp.reshape(b, nch, flat), big_pe(pe_k), big_pe(pe_v),
      wk1[0], wk1[1], wv1[0], wv1[1], bk, bv, big_w2(ck_w2), big_w2(cv_w2))

    n_sel = s // SEL_BLOCK
    key_blk = np.arange(s) // SEL_BLOCK
    onehot_t = jnp.asarray((key_blk[:, None] == np.arange(LANES)[None, :]).astype(np.float32), BF16)
    n_i = np.arange(nch)[None, :]
    j_i = np.arange(n_sel)[:, None]
    overlap_t = ((n_i <= 4 * j_i + 3) & (n_i >= 4 * j_i - 1) & (n_i < nch - 1)).astype(np.float32)
    overlap_t = jnp.asarray(overlap_t, BF16)
    tq = TQ_NSA
    nq = s // tq
    wbias = _window_bias(tq)
    r_i = np.arange(tq)[None, :, None]
    c_i = np.arange(TK_SEL)[None, None, :]
    v_i = np.arange(TK_SEL // tq)[:, None, None]
    cbias = jnp.asarray(np.where(c_i <= v_i * tq + r_i, 0.0, NEG).astype(np.float32))
    onsa = pl.pallas_call(
        functools.partial(_nsa_kernel, seq=s),
        grid=(b, nq),
        in_specs=[pl.BlockSpec((tq, 512), lambda bi, i: (bi * nq + i, 0)),
                  pl.BlockSpec((s, 512), lambda bi, i: (bi, 0)),
                  pl.BlockSpec((None, nch, LANES), lambda bi, i: (bi, 0, 0)),
                  pl.BlockSpec((None, nch, LANES), lambda bi, i: (bi, 0, 0)),
                  pl.BlockSpec((tq, LANES), lambda bi, i: (bi * nq + i, 0)),
                  _const(onehot_t.shape), _const(overlap_t.shape), _const(wbias.shape), _const(cbias.shape)],
        out_specs=pl.BlockSpec((tq, 512), lambda bi, i: (bi * nq + i, 0)),
        out_shape=jax.ShapeDtypeStruct((t, 512), BF16),
        scratch_shapes=[pltpu.VMEM((8 * tq, 2 * LANES), BF16), pltpu.VMEM((8 * tq, LANES), F32),
                        pltpu.VMEM((8 * tq, LANES), F32), pltpu.VMEM((8 * tq, LANES), F32),
                        pltpu.VMEM((8 * tq, LANES), F32)],
        compiler_params=_params(2),
        name="nsa",
    )(qnsa, kvsw, kc, vc, gates, onehot_t, overlap_t, wbias, cbias)

    tq = TQ_MLA
    nq = s // tq
    omla = pl.pallas_call(
        _mla_kernel,
        grid=(b, nq),
        in_specs=[pl.BlockSpec((tq, 1024), lambda bi, i: (bi * nq + i, 0)),
                  pl.BlockSpec((s, 1024), lambda bi, i: (bi, 0)),
                  pl.BlockSpec((s, 512), lambda bi, i: (bi, 0))],
        out_specs=pl.BlockSpec((tq, 512), lambda bi, i: (bi * nq + i, 0)),
        out_shape=jax.ShapeDtypeStruct((t, 512), BF16),
        scratch_shapes=[pltpu.VMEM((MLA_HEADS, tq, LANES), F32), pltpu.VMEM((MLA_HEADS, tq, LANES), F32),
                        pltpu.VMEM((MLA_HEADS, tq, LANES), F32)],
        compiler_params=_params(2),
        name="mla",
    )(qmla, kmla, vmla)

    tm = TM_MERGE
    row_blk = lambda w: pl.BlockSpec((tm, w), lambda i: (i, 0))
    x1 = pl.pallas_call(
        _merge_kernel,
        grid=(t // tm,),
        in_specs=[row_blk(d), row_blk(512), row_blk(512), _const((512, d)), _const((512, d)),
                  _const((d, 2 * d)), _const((d, d)), _const((1, d)), _const((1, d))],
        out_specs=row_blk(d),
        out_shape=jax.ShapeDtypeStruct((t, d), F32),
        compiler_params=_params(1),
        name="merge",
    )(xf, onsa, omla, nsa_w_o[perm, :].astype(BF16), mla_w_o.astype(BF16), wmg, w_out.astype(BF16),
      ln1_g.reshape(1, -1), ln1_b.reshape(1, -1))

    tm = TM_FFN
    nfc = D_FF // FC_FFN
    n_s = s // tm
    conv_tab = jnp.concatenate([conv_w, conv_b.reshape(1, -1), jnp.zeros((4, D_FF), F32)], axis=0)
    blk = pl.BlockSpec((tm, d), lambda bi, i: (bi * n_s + i, 0))
    y = pl.pallas_call(
        _ffn_kernel,
        grid=(b, n_s),
        in_specs=[blk, _const((d, D_FF)), _const((d, D_FF)), _const((D_FF, d)), _const((8, D_FF)),
                  _const((1, d)), _const((1, d))],
        out_specs=blk,
        out_shape=jax.ShapeDtypeStruct((t, d), F32),
        scratch_shapes=[pltpu.VMEM((8, D_FF), F32), pltpu.VMEM((tm, D_FF), BF16)],
        compiler_params=_params(2),
        name="ffn",
    )(x1, w_gate.astype(BF16), w_up.astype(BF16), w_down.astype(BF16), conv_tab,
      ln2_g.reshape(1, -1), ln2_b.reshape(1, -1))
    return y
```

```python
import functools

import numpy as np
import jax
import jax.numpy as jnp
from jax import lax
from jax.experimental import pallas as pl
from jax.experimental.pallas import tpu as pltpu

D_MODEL = 1024
ROPE_THETA = 500000.0
NSA_HEADS = 8
NSA_KV_GROUPS = 2
NSA_GROUP = 4
NSA_HEAD_DIM = 64
NSA_ROT_DIM = 16
CMP_BLOCK = 32
CMP_STRIDE = 16
CMP_HIDDEN = 256
SEL_BLOCK = 64
SEL_TOPK = 8
WINDOW = 256
MLA_HEADS = 8
MLA_Q_RANK = 768
MLA_KV_RANK = 256
MLA_NOPE_DIM = 64
MLA_ROPE_DIM = 32
MLA_V_DIM = 64
MLA_QK_DIM = 96
D_FF = 2816
LN_EPS = 1e-5
RMS_EPS = 1e-6
DEPTH = 1
DEEPNORM_ALPHA = (2 * DEPTH) ** 0.25
LOG2E = 1.4426950408889634

LANES = 128
HALF = LANES // 2
NEG = -1e30
BIG = 3e38
VMEM_LIMIT = 56 * 1024 * 1024

TM_PROJ = 512
TQ_NSA = 256
TK_SEL = 512
TQ_MLA = 512
TK_MLA = 512
TM_MERGE = 512
TM_FFN = 512
FC_FFN = 256

BF16 = jnp.bfloat16
F32 = jnp.float32


def _dot(a, b):
    return jnp.dot(a, b, preferred_element_type=F32)


def _dot_nt(a, b):
    return lax.dot_general(a, b, (((1,), (1,)), ((), ())), preferred_element_type=F32)


def _gelu_tanh(x):
    return x * (0.5 * (1.0 + jnp.tanh(0.7978845608028654 * (x + 0.044715 * (x * x * x)))))


def _sigmoid(x):
    return 1.0 / (1.0 + jnp.exp(-x))


def _layer_norm(x, g, b):
    mu = jnp.mean(x, axis=-1, keepdims=True)
    xc = x - mu
    var = jnp.mean(xc * xc, axis=-1, keepdims=True)
    return xc * lax.rsqrt(var + LN_EPS) * g + b


def _rms_norm(x, g):
    return x * lax.rsqrt(jnp.mean(x * x, axis=-1, keepdims=True) + RMS_EPS) * g


def _rope_chunk(x, cos, sin_signed, first_half, half):
    partner = jnp.where(first_half, pltpu.roll(x, LANES - half, 1), pltpu.roll(x, half, 1))
    return x * cos + partner * sin_signed


def _softmax_rows(s):
    e = jnp.exp2(s - jnp.max(s, axis=-1, keepdims=True))
    return e / jnp.sum(e, axis=-1, keepdims=True)


def _proj_kernel(x_ref, tab_ref, wq_ref, wkv_ref, wg_ref, wcq_ref, wckv_ref, wkr_ref,
                 qn_ref, kvn_ref, wuq_ref, wuk_ref, wuv_ref, place_ref,
                 qnsa_ref, kcmp_ref, vcmp_ref, kvsw_ref, gates_ref, qmla_ref, kmla_ref, vmla_ref):
    xb = x_ref[...].astype(BF16)
    tm = xb.shape[0]
    lane = lax.broadcasted_iota(jnp.int32, (tm, LANES), 1)
    nsa_first = (lane % HALF) < (NSA_ROT_DIM // 2)
    mlaq_first = lane < (MLA_NOPE_DIM + MLA_ROPE_DIM // 2)
    mlak_first = lane < (MLA_ROPE_DIM // 2)
    cn, sn = tab_ref[0], tab_ref[1]
    cq_t, sq_t = tab_ref[2], tab_ref[3]
    ck_t, sk_t = tab_ref[4], tab_ref[5]

    def rope_n(v):
        return _rope_chunk(v, cn, sn, nsa_first, NSA_ROT_DIM // 2)

    q = _dot(xb, wq_ref[...])
    kv = _dot(xb, wkv_ref[...])
    cq_raw = _dot(xb, wcq_ref[...])
    ckv_raw = _dot(xb, wckv_ref[...])
    kr_raw = _dot(xb, wkr_ref[...])
    gate_logits = _dot(xb, wg_ref[...])

    cq = _rms_norm(cq_raw, qn_ref[...]).astype(BF16)
    qm = _dot(cq, wuq_ref[...])
    ckv = _rms_norm(ckv_raw, kvn_ref[...]).astype(BF16)
    kr = _rope_chunk(kr_raw, ck_t, sk_t, mlak_first, MLA_ROPE_DIM // 2).astype(BF16)
    kmla_ref[...] = (_dot(ckv, wuk_ref[...]) + _dot(kr, place_ref[...])).astype(BF16)
    vmla_ref[...] = _dot(ckv, wuv_ref[...]).astype(BF16)

    for c in range(4):
        sl = slice(c * LANES, (c + 1) * LANES)
        qnsa_ref[:, sl] = (rope_n(q[:, sl]) * (NSA_HEAD_DIM ** -0.5 * LOG2E)).astype(BF16)

    kcmp_ref[...] = rope_n(kv[:, 0:128])
    vcmp_ref[...] = kv[:, 128:256]
    kvsw_ref[:, 0:128] = rope_n(kv[:, 256:384]).astype(BF16)
    kvsw_ref[:, 128:256] = kv[:, 384:512].astype(BF16)
    kvsw_ref[:, 256:384] = rope_n(kv[:, 512:640]).astype(BF16)
    kvsw_ref[:, 384:512] = kv[:, 640:768].astype(BF16)

    gates_ref[...] = _sigmoid(gate_logits)

    for h in range(MLA_HEADS):
        sl = slice(h * LANES, (h + 1) * LANES)
        roped = _rope_chunk(qm[:, sl], cq_t, sq_t, mlaq_first, MLA_ROPE_DIM // 2)
        qmla_ref[:, sl] = (roped * (MLA_QK_DIM ** -0.5 * LOG2E)).astype(BF16)


def _compress_kernel(kc_ref, vc_ref, pek_ref, pev_ref, wka_ref, wkb_ref, wva_ref, wvb_ref,
                     bk_ref, bv_ref, wk2_ref, wv2_ref, kout_ref, vout_ref):
    def one(tok_ref, pe_ref, wa_ref, wb_ref, b_ref, w2_ref, out_ref):
        nch = tok_ref.shape[0] // CMP_STRIDE
        ch = jnp.concatenate([tok_ref[pl.ds(l, nch, stride=CMP_STRIDE), :] for l in range(CMP_STRIDE)], axis=1)
        nxt = pltpu.roll(ch, nch - 1, 0)
        a = (ch + pe_ref[0:1, :]).astype(BF16)
        b = (nxt + pe_ref[1:2, :]).astype(BF16)
        h = _dot(a, wa_ref[...]) + _dot(b, wb_ref[...]) + b_ref[...]
        out_ref[...] = _dot(_gelu_tanh(h).astype(BF16), w2_ref[...]).astype(BF16)

    one(kc_ref, pek_ref, wka_ref, wkb_ref, bk_ref, wk2_ref, kout_ref)
    one(vc_ref, pev_ref, wva_ref, wvb_ref, bv_ref, wv2_ref, vout_ref)


def _nsa_kernel(q_ref, kvsw_ref, kc_ref, vc_ref, gates_ref, et_ref, overlap_t_ref, wbias_ref, cbias_ref, o_ref,
                qaug_scr, m_scr, acc_scr, part_scr, sa_scr, sb_scr, *, seq):
    tq = TQ_NSA
    rows = 8 * tq
    n_blk = seq // SEL_BLOCK
    i = pl.program_id(1)
    s0 = i * tq
    q = q_ref[...]
    lane_q = lax.broadcasted_iota(jnp.int32, (tq, LANES), 1)
    lo = lane_q < HALF
    zero = jnp.zeros((tq, LANES), BF16)
    for half in range(2):
        for c in range(4):
            r = 4 * half + c
            chunk = q[:, c * LANES:(c + 1) * LANES]
            qaug_scr[r * tq:(r + 1) * tq, 0:LANES] = jnp.where(lo if half == 0 else jnp.logical_not(lo), chunk, zero)
    qs = qaug_scr[:, 0:LANES]

    ncmp = kc_ref.shape[0]
    wk = WINDOW + tq
    w0 = pl.multiple_of(jnp.maximum(s0 - WINDOW, 0), tq)
    sc = _dot_nt(qs, kc_ref[...])
    sw = _dot_nt(qs, kvsw_ref[pl.ds(w0, wk), 256:384])
    n_idx = lax.broadcasted_iota(jnp.int32, (rows, ncmp), 1)
    pos_c = s0 + (lax.broadcasted_iota(jnp.int32, (rows, ncmp), 0) & (tq - 1))
    cmask = n_idx * CMP_STRIDE + (CMP_BLOCK - 1) <= pos_c
    sm = jnp.where(cmask, sc, NEG)
    e = jnp.where(cmask, jnp.exp2(sm - jnp.max(sm, axis=-1, keepdims=True)), 0.0)
    p_cmp = e / jnp.maximum(jnp.sum(e, axis=-1, keepdims=True), 1e-30)
    p_blocks = [p_cmp[r * tq:(r + 1) * tq] for r in range(8)]
    o_cmp = _dot(p_cmp.astype(BF16), vc_ref[...])
    wbias = wbias_ref[jnp.minimum(i, WINDOW // tq)]
    p_win = _softmax_rows(jnp.concatenate([sw[r * tq:(r + 1) * tq] + wbias for r in range(8)], axis=0))
    o_win = _dot(p_win.astype(BF16), kvsw_ref[pl.ds(w0, wk), 384:512])
    gates = gates_ref[...]
    for r in range(8):
        sl = slice(r * tq, (r + 1) * tq)
        part_scr[sl] = gates[:, r:r + 1] * o_cmp[sl] + gates[:, 16 + r:17 + r] * o_win[sl]

    blk = lax.broadcasted_iota(jnp.int32, (n_blk, LANES), 0)
    pad = jnp.zeros((LANES - n_blk, LANES), F32)
    imps = []
    for g in range(2):
        psum = (p_blocks[4 * g] + p_blocks[4 * g + 1]) + (p_blocks[4 * g + 2] + p_blocks[4 * g + 3])
        hi = psum.astype(BF16)
        lo_part = (psum - hi.astype(F32)).astype(BF16)
        imps.append(_dot_nt(overlap_t_ref[...], hi) + _dot_nt(overlap_t_ref[...], lo_part))
    for g, part in [(g, part) for g in range(2) for part in range(tq // LANES)]:
        cur = (s0 + part * LANES + lax.broadcasted_iota(jnp.int32, (n_blk, LANES), 1)) // SEL_BLOCK
        valid = blk <= cur
        forced = (blk == 0) | (blk == cur) | (blk == cur - 1)
        imp = imps[g][:, part * LANES:(part + 1) * LANES]
        impv = jnp.where(valid, jnp.where(forced, BIG, imp), -1.0)
        n_slab = n_blk // 8
        slabs = [impv[8 * v:8 * (v + 1)] for v in range(n_slab)]
        cnts = [jnp.zeros((8, LANES), F32) for _ in range(n_slab)]
        for b in range(n_blk):
            other = impv[b:b + 1, :]
            for v in range(n_slab):
                if b < 8 * v:
                    beats = other >= slabs[v]
                elif b >= 8 * (v + 1):
                    beats = other > slabs[v]
                else:
                    beats = (other > slabs[v]) | ((other == slabs[v]) & (blk[0:8] > b - 8 * v))
                cnts[v] = cnts[v] + jnp.where(beats, 1.0, 0.0)
        cnt = jnp.concatenate(cnts, axis=0)
        bias_t = jnp.where(valid & (cnt < float(SEL_TOPK)), 0.0, NEG)
        bias = jnp.concatenate([bias_t, pad], axis=0).T.astype(BF16)
        for r in range(4 * g, 4 * g + 4):
            qaug_scr[r * tq + part * LANES:r * tq + (part + 1) * LANES, LANES:2 * LANES] = bias

    m_scr[...] = jnp.full((rows, LANES), NEG, F32)
    acc_scr[...] = jnp.zeros((rows, LANES), F32)

    def sel_scores(kt, buf):
        k0 = pl.multiple_of(kt * TK_SEL, TK_SEL)
        k_aug = jnp.concatenate([kvsw_ref[pl.ds(k0, TK_SEL), 0:128], et_ref[pl.ds(k0, TK_SEL), :]], axis=1)
        buf[...] = _dot_nt(qaug_scr[...], k_aug)

    def sel_tile(kt, buf, diagonal):
        k0 = pl.multiple_of(kt * TK_SEL, TK_SEL)
        v = kvsw_ref[pl.ds(k0, TK_SEL), 128:256]
        s = buf[...]
        if diagonal:
            cb = cbias_ref[(s0 - k0) // tq]
            s = jnp.concatenate([s[r * tq:(r + 1) * tq] + cb for r in range(8)], axis=0)
        m_prev = m_scr[...]
        m_new = jnp.maximum(m_prev, jnp.max(s, axis=-1, keepdims=True))
        alpha = jnp.exp2(m_prev - m_new)
        e = jnp.exp2((s - jnp.tile(m_new, (1, TK_SEL // LANES))).astype(BF16))
        v_lane = lax.broadcasted_iota(jnp.int32, v.shape, 1)
        one = jnp.ones(v.shape, BF16)
        pv = jnp.concatenate([_dot(e[:rows // 2], jnp.where(v_lane < HALF, v, one)),
                              _dot(e[rows // 2:], jnp.where(v_lane < HALF, one, v))], axis=0)
        acc_scr[...] = alpha * acc_scr[...] + pv
        m_scr[...] = m_new

    last = s0 // TK_SEL
    sel_scores(0, sa_scr)

    def tile_pair(p, carry):
        kt = 2 * p
        sel_scores(kt + 1, sb_scr)
        sel_tile(kt, sa_scr, False)
        sel_scores(kt + 2, sa_scr)
        sel_tile(kt + 1, sb_scr, False)
        return carry

    lax.fori_loop(0, last // 2, tile_pair, 0)
    last_is_odd = (last & 1) == 1

    @pl.when(last_is_odd)
    def _():
        sel_scores(last, sb_scr)
        sel_tile(last - 1, sa_scr, False)
        sel_tile(last, sb_scr, True)

    @pl.when(jnp.logical_not(last_is_odd))
    def _():
        sel_tile(last, sa_scr, True)

    acc = acc_scr[...]
    o_sel = acc / pltpu.roll(acc, HALF, 1)

    gates = gates_ref[...]
    outs = []
    for r in range(8):
        sl = slice(r * tq, (r + 1) * tq)
        outs.append(part_scr[sl] + gates[:, 8 + r:9 + r] * o_sel[sl])
    for c in range(4):
        o_ref[:, c * LANES:(c + 1) * LANES] = jnp.where(lo, outs[c], outs[4 + c]).astype(BF16)


def _mla_kernel(q_ref, k_ref, v_ref, o_ref, m_scr, acc_scr):
    tq = TQ_MLA
    i = pl.program_id(1)
    s0 = i * tq
    m_scr[...] = jnp.full(m_scr.shape, NEG, F32)
    acc_scr[...] = jnp.zeros(acc_scr.shape, F32)

    def tile(kt, diagonal):
        k0 = pl.multiple_of(kt * TK_MLA, TK_MLA)
        if diagonal:
            kpos = k0 + lax.broadcasted_iota(jnp.int32, (tq, TK_MLA), 1)
            causal = kpos <= s0 + lax.broadcasted_iota(jnp.int32, (tq, TK_MLA), 0)
        scores = [_dot_nt(q_ref[:, h * LANES:(h + 1) * LANES], k_ref[pl.ds(k0, TK_MLA), h * LANES:(h + 1) * LANES])
                  for h in range(MLA_HEADS)]
        v_lane = lax.broadcasted_iota(jnp.int32, (TK_MLA, LANES), 1)
        one = jnp.ones((TK_MLA, LANES), BF16)
        for h in range(MLA_HEADS):
            v2 = v_ref[pl.ds(k0, TK_MLA), (h // 2) * LANES:(h // 2 + 1) * LANES]
            vh = jnp.where(v_lane < HALF, v2, one) if h % 2 == 0 else jnp.where(v_lane < HALF, one, v2)
            s = jnp.where(causal, scores[h], NEG) if diagonal else scores[h]
            m_prev = m_scr[h]
            m_new = jnp.maximum(m_prev, jnp.max(s, axis=-1, keepdims=True))
            alpha = jnp.exp2(m_prev - m_new)
            e = jnp.exp2((s - jnp.tile(m_new, (1, TK_MLA // LANES))).astype(BF16))
            acc_scr[h] = alpha * acc_scr[h] + _dot(e, vh)
            m_scr[h] = m_new

    last = s0 // TK_MLA

    def full_tile(kt, carry):
        tile(kt, False)
        return carry

    lax.fori_loop(0, last, full_tile, 0)
    tile(last, True)
    lo = lax.broadcasted_iota(jnp.int32, (tq, LANES), 1) < HALF
    for c in range(4):
        mixed = jnp.where(lo, acc_scr[2 * c], acc_scr[2 * c + 1])
        denom = jnp.where(lo, acc_scr[2 * c + 1], acc_scr[2 * c])
        o_ref[:, c * LANES:(c + 1) * LANES] = (mixed / pltpu.roll(denom, HALF, 1)).astype(BF16)


def _merge_kernel(x_ref, onsa_ref, omla_ref, wa_ref, wb_ref, wmg_ref, wout_ref, g_ref, b_ref, y_ref):
    n_sub = 4
    sub = x_ref.shape[0] // n_sub
    first = []
    for j in range(n_sub):
        rs = slice(j * sub, (j + 1) * sub)
        first.append((_dot(onsa_ref[rs, :], wa_ref[...]), _dot(omla_ref[rs, :], wb_ref[...]),
                      _dot(x_ref[rs, :].astype(BF16), wmg_ref[...])))
    for j in range(n_sub):
        rs = slice(j * sub, (j + 1) * sub)
        ya, yb, mg_logits = first[j]
        mg = _sigmoid(mg_logits)
        mixed = (mg[:, :D_MODEL] * ya + mg[:, D_MODEL:] * yb).astype(BF16)
        y_ref[rs, :] = _layer_norm(DEEPNORM_ALPHA * x_ref[rs, :] + _dot(mixed, wout_ref[...]), g_ref[...], b_ref[...])


def _ffn_kernel(x_ref, wg_ref, wu_ref, wd_ref, cw_ref, g_ref, b_ref, y_ref, tail_scr, h_scr):
    tm = TM_FFN
    nfc = D_FF // FC_FFN
    first_tile = pl.program_id(1) == 0
    x = x_ref[...]
    xb = x.astype(BF16)
    row = lax.broadcasted_iota(jnp.int32, (tm, FC_FFN), 0)

    @pl.when(first_tile)
    def _():
        tail_scr[...] = jnp.zeros(tail_scr.shape, F32)

    def gate_up(fc):
        cols = slice(fc * FC_FFN, (fc + 1) * FC_FFN)
        return _dot(xb, wg_ref[:, cols]), _dot(xb, wu_ref[:, cols])

    cur = gate_up(0)
    for fc in range(nfc):
        cols = slice(fc * FC_FFN, (fc + 1) * FC_FFN)
        nxt = gate_up(fc + 1) if fc + 1 < nfc else None
        a, up = cur
        tail = tail_scr[:, cols]
        tail_scr[:, cols] = a[tm - 8:, :]
        a1 = jnp.where(row == 0, tail[7:8, :], pltpu.roll(a, 1, 0))
        a2 = jnp.where(row == 0, tail[6:7, :], jnp.where(row == 1, tail[7:8, :], pltpu.roll(a, 2, 0)))
        cw = cw_ref[:, cols]
        conv = cw[0:1, :] * a2 + cw[1:2, :] * a1 + cw[2:3, :] * a + cw[3:4, :]
        h_scr[:, cols] = (_gelu_tanh(conv) * up).astype(BF16)
        cur = nxt
    half = tm // 2
    ys = [_dot(h_scr[j * half:(j + 1) * half, :], wd_ref[...]) for j in range(2)]
    for j in range(2):
        rs = slice(j * half, (j + 1) * half)
        y_ref[rs, :] = _layer_norm(DEEPNORM_ALPHA * x_ref[rs, :] + ys[j], g_ref[...], b_ref[...])


def _rope_tables(seq):
    pos = jnp.arange(seq, dtype=F32)

    def cs(rot):
        inv = ROPE_THETA ** (-jnp.arange(0, rot, 2, dtype=F32) / rot)
        ang = pos[:, None] * inv[None, :]
        return jnp.cos(ang), jnp.sin(ang)

    one = lambda n: jnp.ones((seq, n), F32)
    zero = lambda n: jnp.zeros((seq, n), F32)
    c8, s8 = cs(NSA_ROT_DIM)
    c16, s16 = cs(MLA_ROPE_DIM)
    nsa_c = jnp.concatenate([c8, c8, one(48)] * 2, axis=1)
    nsa_s = jnp.concatenate([-s8, s8, zero(48)] * 2, axis=1)
    mq_c = jnp.concatenate([one(64), c16, c16, one(32)], axis=1)
    mq_s = jnp.concatenate([zero(64), -s16, s16, zero(32)], axis=1)
    mk_c = jnp.concatenate([c16, c16, one(96)], axis=1)
    mk_s = jnp.concatenate([-s16, s16, zero(96)], axis=1)
    return jnp.stack([nsa_c, nsa_s, mq_c, mq_s, mk_c, mk_s])


def _head_perm():
    return np.array([(c + 4 * half) * 64 + d for c in range(4) for half in range(2) for d in range(64)])


def _window_bias(tq):
    wk = WINDOW + tq
    out = np.zeros((WINDOW // tq + 1, tq, wk), np.float32)
    for v in range(WINDOW // tq + 1):
        s0 = v * tq
        w0 = max(s0 - WINDOW, 0)
        pos = s0 + np.arange(tq)[:, None]
        kpos = w0 + np.arange(wk)[None, :]
        out[v] = np.where((kpos <= pos) & (pos - kpos < WINDOW), 0.0, NEG)
    return jnp.asarray(out)


def _const(shape):
    nd = len(shape)
    return pl.BlockSpec(shape, lambda *_: (0,) * nd, pipeline_mode=pl.Buffered(1))


def _params(n_axes):
    return pltpu.CompilerParams(dimension_semantics=("arbitrary",) * n_axes, vmem_limit_bytes=VMEM_LIMIT)


def kernel(x, w_in, cmp_pe_k, cmp_pe_v, cmp_k_w1, cmp_k_b1, cmp_k_w2, cmp_v_w1, cmp_v_b1, cmp_v_w2,
           nsa_w_o, mla_q_norm, mla_w_uq, mla_kv_norm, mla_w_uk, mla_w_uv, mla_w_o, w_out,
           ln1_g, ln1_b, ffn_w_gate, ffn_w_up, ffn_conv_w, ffn_conv_b, ffn_w_down, ln2_g, ln2_b):
    b, s, d = x.shape
    assert d == D_MODEL and s % TK_SEL == 0 and s % TM_FFN == 0 and (s // CMP_STRIDE) % 8 == 0
    assert WINDOW % TQ_NSA == 0 and s >= WINDOW + TQ_NSA
    t = b * s
    perm = _head_perm()
    xf = x.reshape(t, d)
    for l in range(DEPTH):
        xf = _layer(xf, b, s, perm, w_in[l], cmp_pe_k[l], cmp_pe_v[l], cmp_k_w1[l], cmp_k_b1[l], cmp_k_w2[l],
                    cmp_v_w1[l], cmp_v_b1[l], cmp_v_w2[l], nsa_w_o[l], mla_q_norm[l], mla_w_uq[l],
                    mla_kv_norm[l], mla_w_uk[l], mla_w_uv[l], mla_w_o[l], w_out[l], ln1_g[l], ln1_b[l],
                    ffn_w_gate[l], ffn_w_up[l], ffn_conv_w[l], ffn_conv_b[l], ffn_w_down[l], ln2_g[l], ln2_b[l])
    return xf.reshape(b, s, d)


def _layer(xf, b, s, perm, w_in, pe_k, pe_v, ck_w1, ck_b1, ck_w2, cv_w1, cv_b1, cv_w2, nsa_w_o, q_norm, w_uq,
           kv_norm, w_uk, w_uv, mla_w_o, w_out, ln1_g, ln1_b, w_gate, w_up, conv_w, conv_b, w_down, ln2_g, ln2_b):
    t = b * s
    d = D_MODEL
    o = np.cumsum([0, 512, 128, 128, 128, 128, 128, 128, 24, MLA_Q_RANK, MLA_KV_RANK, MLA_ROPE_DIM, 2 * d])
    wq = w_in[:, o[0]:o[1]][:, perm].astype(BF16)
    wkv = w_in[:, o[1]:o[7]].astype(BF16)
    gate_cols = np.array([h * 3 + br for br in range(3) for h in range(8)])
    wg = jnp.pad(w_in[:, o[7]:o[8]][:, gate_cols], ((0, 0), (0, LANES - 24))).astype(BF16)
    wcq = w_in[:, o[8]:o[9]].astype(BF16)
    wckv = w_in[:, o[9]:o[10]].astype(BF16)
    wkr = jnp.pad(w_in[:, o[10]:o[11]], ((0, 0), (0, LANES - MLA_ROPE_DIM))).astype(BF16)
    wmg = w_in[:, o[11]:o[12]].astype(BF16)

    wuq = jnp.pad(w_uq.reshape(MLA_Q_RANK, MLA_HEADS, MLA_QK_DIM), ((0, 0), (0, 0), (0, LANES - MLA_QK_DIM)))
    wuq = wuq.reshape(MLA_Q_RANK, MLA_HEADS * LANES).astype(BF16)
    wuk = jnp.pad(w_uk.reshape(MLA_KV_RANK, MLA_HEADS, MLA_NOPE_DIM), ((0, 0), (0, 0), (0, LANES - MLA_NOPE_DIM)))
    wuk = wuk.reshape(MLA_KV_RANK, MLA_HEADS * LANES).astype(BF16)
    place = np.zeros((LANES, MLA_HEADS * LANES), np.float32)
    for h in range(MLA_HEADS):
        for r in range(MLA_ROPE_DIM):
            place[r, h * LANES + MLA_NOPE_DIM + r] = 1.0
    place = jnp.asarray(place, BF16)
    tabs = _rope_tables(s)

    tm = TM_PROJ
    n_s = s // tm
    row_blk = lambda w: pl.BlockSpec((tm, w), lambda i: (i, 0))
    outs = pl.pallas_call(
        _proj_kernel,
        grid=(t // tm,),
        in_specs=[row_blk(d), pl.BlockSpec((6, tm, LANES), lambda i: (0, i % n_s, 0)),
                  _const(wq.shape), _const(wkv.shape), _const(wg.shape), _const(wcq.shape), _const(wckv.shape),
                  _const(wkr.shape), _const((1, MLA_Q_RANK)), _const((1, MLA_KV_RANK)), _const(wuq.shape),
                  _const(wuk.shape), _const((MLA_KV_RANK, 512)), _const(place.shape)],
        out_specs=[row_blk(512), row_blk(128), row_blk(128), row_blk(512), row_blk(128),
                   row_blk(1024), row_blk(1024), row_blk(512)],
        out_shape=[jax.ShapeDtypeStruct((t, 512), BF16), jax.ShapeDtypeStruct((t, 128), F32),
                   jax.ShapeDtypeStruct((t, 128), F32), jax.ShapeDtypeStruct((t, 512), BF16),
                   jax.ShapeDtypeStruct((t, 128), F32), jax.ShapeDtypeStruct((t, 1024), BF16),
                   jax.ShapeDtypeStruct((t, 1024), BF16), jax.ShapeDtypeStruct((t, 512), BF16)],
        compiler_params=_params(1),
        name="proj",
    )(xf, tabs, wq, wkv, wg, wcq, wckv, wkr, q_norm.reshape(1, -1), kv_norm.reshape(1, -1), wuq, wuk,
      w_uv.astype(BF16), place)
    qnsa, kcmp, vcmp, kvsw, gates, qmla, kmla, vmla = outs

    nch = s // CMP_STRIDE
    flat = CMP_STRIDE * LANES

    def big_w1(w1):
        w = w1.reshape(2, CMP_STRIDE, NSA_HEAD_DIM, CMP_HIDDEN)
        eye = jnp.eye(2, dtype=F32)
        return jnp.einsum('hldj,ge->hlgdej', w, eye).reshape(2, flat, 2 * CMP_HIDDEN).astype(BF16)

    def big_pe(pe):
        return jnp.tile(pe.reshape(2, CMP_STRIDE, 1, NSA_HEAD_DIM), (1, 1, 2, 1)).reshape(2, flat)

    def big_w2(w2):
        return jnp.einsum('jd,ge->gjed', w2, jnp.eye(2, dtype=F32)).reshape(2 * CMP_HIDDEN, LANES).astype(BF16)

    wk1, wv1 = big_w1(ck_w1), big_w1(cv_w1)
    bk = jnp.tile(ck_b1, 2).reshape(1, -1)
    bv = jnp.tile(cv_b1, 2).reshape(1, -1)
    tok_blk = pl.BlockSpec((s, LANES), lambda i: (i, 0))
    cmp_blk = pl.BlockSpec((None, nch, LANES), lambda i: (i, 0, 0))
    kc, vc = pl.pallas_call(
        _compress_kernel,
        grid=(b,),
        in_specs=[tok_blk, tok_blk, _const((2, flat)), _const((2, flat)),
                  _const((flat, 512)), _const((flat, 512)), _const((flat, 512)), _const((flat, 512)),
                  _const((1, 512)), _const((1, 512)), _const((512, LANES)), _const((512, LANES))],
        out_specs=[cmp_blk, cmp_blk],
        out_shape=[jax.ShapeDtypeStruct((b, nch, LANES), BF16)] * 2,
        compiler_params=_params(1),
        name="compress",
    )(kcmp, vcmp, big_pe(pe_k), big_pe(pe_v),
      wk1[0], wk1[1], wv1[0], wv1[1], bk, bv, big_w2(ck_w2), big_w2(cv_w2))

    n_sel = s // SEL_BLOCK
    key_blk = np.arange(s) // SEL_BLOCK
    onehot_t = jnp.asarray((key_blk[:, None] == np.arange(LANES)[None, :]).astype(np.float32), BF16)
    n_i = np.arange(nch)[None, :]
    j_i = np.arange(n_sel)[:, None]
    overlap_t = ((n_i <= 4 * j_i + 3) & (n_i >= 4 * j_i - 1) & (n_i < nch - 1)).astype(np.float32)
    overlap_t = jnp.asarray(overlap_t, BF16)
    tq = TQ_NSA
    nq = s // tq
    wbias = _window_bias(tq)
    r_i = np.arange(tq)[None, :, None]
    c_i = np.arange(TK_SEL)[None, None, :]
    v_i = np.arange(TK_SEL // tq)[:, None, None]
    cbias = jnp.asarray(np.where(c_i <= v_i * tq + r_i, 0.0, NEG).astype(np.float32))
    onsa = pl.pallas_call(
        functools.partial(_nsa_kernel, seq=s),
        grid=(b, nq),
        in_specs=[pl.BlockSpec((tq, 512), lambda bi, i: (bi * nq + i, 0)),
                  pl.BlockSpec((s, 512), lambda bi, i: (bi, 0)),
                  pl.BlockSpec((None, nch, LANES), lambda bi, i: (bi, 0, 0)),
                  pl.BlockSpec((None, nch, LANES), lambda bi, i: (bi, 0, 0)),
                  pl.BlockSpec((tq, LANES), lambda bi, i: (bi * nq + i, 0)),
                  _const(onehot_t.shape), _const(overlap_t.shape), _const(wbias.shape), _const(cbias.shape)],
        out_specs=pl.BlockSpec((tq, 512), lambda bi, i: (bi * nq + i, 0)),
        out_shape=jax.ShapeDtypeStruct((t, 512), BF16),
        scratch_shapes=[pltpu.VMEM((8 * tq, 2 * LANES), BF16), pltpu.VMEM((8 * tq, LANES), F32),
                        pltpu.VMEM((8 * tq, LANES), F32),
                        pltpu.VMEM((8 * tq, LANES), F32), pltpu.VMEM((8 * tq, TK_SEL), F32),
                        pltpu.VMEM((8 * tq, TK_SEL), F32)],
        compiler_params=_params(2),
        name="nsa",
    )(qnsa, kvsw, kc, vc, gates, onehot_t, overlap_t, wbias, cbias)

    tq = TQ_MLA
    nq = s // tq
    omla = pl.pallas_call(
        _mla_kernel,
        grid=(b, nq),
        in_specs=[pl.BlockSpec((tq, 1024), lambda bi, i: (bi * nq + i, 0)),
                  pl.BlockSpec((s, 1024), lambda bi, i: (bi, 0)),
                  pl.BlockSpec((s, 512), lambda bi, i: (bi, 0))],
        out_specs=pl.BlockSpec((tq, 512), lambda bi, i: (bi * nq + i, 0)),
        out_shape=jax.ShapeDtypeStruct((t, 512), BF16),
        scratch_shapes=[pltpu.VMEM((MLA_HEADS, tq, LANES), F32), pltpu.VMEM((MLA_HEADS, tq, LANES), F32)],
        compiler_params=_params(2),
        name="mla",
    )(qmla, kmla, vmla)

    tm = TM_MERGE
    row_blk = lambda w: pl.BlockSpec((tm, w), lambda i: (i, 0))
    x1 = pl.pallas_call(
        _merge_kernel,
        grid=(t // tm,),
        in_specs=[row_blk(d), row_blk(512), row_blk(512), _const((512, d)), _const((512, d)),
                  _const((d, 2 * d)), _const((d, d)), _const((1, d)), _const((1, d))],
        out_specs=row_blk(d),
        out_shape=jax.ShapeDtypeStruct((t, d), F32),
        compiler_params=_params(1),
        name="merge",
    )(xf, onsa, omla, nsa_w_o[perm, :].astype(BF16), mla_w_o.astype(BF16), wmg, w_out.astype(BF16),
      ln1_g.reshape(1, -1), ln1_b.reshape(1, -1))

    tm = TM_FFN
    nfc = D_FF // FC_FFN
    n_s = s // tm
    conv_tab = jnp.concatenate([conv_w, conv_b.reshape(1, -1), jnp.zeros((4, D_FF), F32)], axis=0)
    blk = pl.BlockSpec((tm, d), lambda bi, i: (bi * n_s + i, 0))
    y = pl.pallas_call(
        _ffn_kernel,
        grid=(b, n_s),
        in_specs=[blk, _const((d, D_FF)), _const((d, D_FF)), _const((D_FF, d)), _const((8, D_FF)),
                  _const((1, d)), _const((1, d))],
        out_specs=blk,
        out_shape=jax.ShapeDtypeStruct((t, d), F32),
        scratch_shapes=[pltpu.VMEM((8, D_FF), F32), pltpu.VMEM((tm, D_FF), BF16)],
        compiler_params=_params(2),
        name="ffn",
    )(x1, w_gate.astype(BF16), w_up.astype(BF16), w_down.astype(BF16), conv_tab,
      ln2_g.reshape(1, -1), ln2_b.reshape(1, -1))
    return y
```

```python
import functools

import numpy as np
import jax
import jax.numpy as jnp
from jax import lax
from jax.experimental import pallas as pl
from jax.experimental.pallas import tpu as pltpu

D_MODEL = 1024
ROPE_THETA = 500000.0
NSA_HEADS = 8
NSA_KV_GROUPS = 2
NSA_GROUP = 4
NSA_HEAD_DIM = 64
NSA_ROT_DIM = 16
CMP_BLOCK = 32
CMP_STRIDE = 16
CMP_HIDDEN = 256
SEL_BLOCK = 64
SEL_TOPK = 8
WINDOW = 256
MLA_HEADS = 8
MLA_Q_RANK = 768
MLA_KV_RANK = 256
MLA_NOPE_DIM = 64
MLA_ROPE_DIM = 32
MLA_V_DIM = 64
MLA_QK_DIM = 96
D_FF = 2816
LN_EPS = 1e-5
RMS_EPS = 1e-6
DEPTH = 1
DEEPNORM_ALPHA = (2 * DEPTH) ** 0.25
LOG2E = 1.4426950408889634

LANES = 128
HALF = LANES // 2
NEG = -1e30
BIG = 3e38
VMEM_LIMIT = 56 * 1024 * 1024

TM_PROJ = 512
TQ_NSA = 256
TK_SEL = 512
TQ_MLA = 512
TK_MLA = 512
TM_MERGE = 512
TM_FFN = 512
FC_FFN = 256

BF16 = jnp.bfloat16
F32 = jnp.float32


def _dot(a, b):
    return jnp.dot(a, b, preferred_element_type=F32)


def _dot_nt(a, b):
    return lax.dot_general(a, b, (((1,), (1,)), ((), ())), preferred_element_type=F32)


def _gelu_tanh(x):
    return x * (0.5 * (1.0 + jnp.tanh(0.7978845608028654 * (x + 0.044715 * (x * x * x)))))


def _sigmoid(x):
    return 1.0 / (1.0 + jnp.exp(-x))


def _layer_norm(x, g, b):
    mu = jnp.mean(x, axis=-1, keepdims=True)
    xc = x - mu
    var = jnp.mean(xc * xc, axis=-1, keepdims=True)
    return xc * lax.rsqrt(var + LN_EPS) * g + b


def _rms_norm(x, g):
    return x * lax.rsqrt(jnp.mean(x * x, axis=-1, keepdims=True) + RMS_EPS) * g


def _rope_chunk(x, cos, sin_signed, first_half, half):
    partner = jnp.where(first_half, pltpu.roll(x, LANES - half, 1), pltpu.roll(x, half, 1))
    return x * cos + partner * sin_signed


def _softmax_rows(s):
    e = jnp.exp2(s - jnp.max(s, axis=-1, keepdims=True))
    return e / jnp.sum(e, axis=-1, keepdims=True)


def _proj_kernel(x_ref, tab_ref, wq_ref, wkv_ref, wg_ref, wcq_ref, wckv_ref, wkr_ref,
                 qn_ref, kvn_ref, wuq_ref, wuk_ref, wuv_ref, place_ref,
                 qnsa_ref, kcmp_ref, vcmp_ref, kvsw_ref, gates_ref, qmla_ref, kmla_ref, vmla_ref):
    xb = x_ref[...].astype(BF16)
    tm = xb.shape[0]
    lane = lax.broadcasted_iota(jnp.int32, (tm, LANES), 1)
    nsa_first = (lane % HALF) < (NSA_ROT_DIM // 2)
    mlaq_first = lane < (MLA_NOPE_DIM + MLA_ROPE_DIM // 2)
    mlak_first = lane < (MLA_ROPE_DIM // 2)
    cn, sn = tab_ref[0], tab_ref[1]
    cq_t, sq_t = tab_ref[2], tab_ref[3]
    ck_t, sk_t = tab_ref[4], tab_ref[5]

    def rope_n(v):
        return _rope_chunk(v, cn, sn, nsa_first, NSA_ROT_DIM // 2)

    q = _dot(xb, wq_ref[...])
    kv = _dot(xb, wkv_ref[...])
    cq_raw = _dot(xb, wcq_ref[...])
    ckv_raw = _dot(xb, wckv_ref[...])
    kr_raw = _dot(xb, wkr_ref[...])
    gate_logits = _dot(xb, wg_ref[...])

    cq = _rms_norm(cq_raw, qn_ref[...]).astype(BF16)
    qm = _dot(cq, wuq_ref[...])
    ckv = _rms_norm(ckv_raw, kvn_ref[...]).astype(BF16)
    kr = _rope_chunk(kr_raw, ck_t, sk_t, mlak_first, MLA_ROPE_DIM // 2).astype(BF16)
    kmla_ref[...] = (_dot(ckv, wuk_ref[...]) + _dot(kr, place_ref[...])).astype(BF16)
    vmla_ref[...] = _dot(ckv, wuv_ref[...]).astype(BF16)

    for c in range(4):
        sl = slice(c * LANES, (c + 1) * LANES)
        qnsa_ref[:, sl] = (rope_n(q[:, sl]) * (NSA_HEAD_DIM ** -0.5 * LOG2E)).astype(BF16)

    kcmp_ref[...] = rope_n(kv[:, 0:128])
    vcmp_ref[...] = kv[:, 128:256]
    kvsw_ref[:, 0:128] = rope_n(kv[:, 256:384]).astype(BF16)
    kvsw_ref[:, 128:256] = kv[:, 384:512].astype(BF16)
    kvsw_ref[:, 256:384] = rope_n(kv[:, 512:640]).astype(BF16)
    kvsw_ref[:, 384:512] = kv[:, 640:768].astype(BF16)

    gates_ref[...] = _sigmoid(gate_logits)

    for h in range(MLA_HEADS):
        sl = slice(h * LANES, (h + 1) * LANES)
        roped = _rope_chunk(qm[:, sl], cq_t, sq_t, mlaq_first, MLA_ROPE_DIM // 2)
        qmla_ref[:, sl] = (roped * (MLA_QK_DIM ** -0.5 * LOG2E)).astype(BF16)


def _compress_kernel(kc_ref, vc_ref, pek_ref, pev_ref, wka_ref, wkb_ref, wva_ref, wvb_ref,
                     bk_ref, bv_ref, wk2_ref, wv2_ref, kout_ref, vout_ref):
    def one(tok_ref, pe_ref, wa_ref, wb_ref, b_ref, w2_ref, out_ref):
        nch = tok_ref.shape[0] // CMP_STRIDE
        ch = jnp.concatenate([tok_ref[pl.ds(l, nch, stride=CMP_STRIDE), :] for l in range(CMP_STRIDE)], axis=1)
        nxt = pltpu.roll(ch, nch - 1, 0)
        a = (ch + pe_ref[0:1, :]).astype(BF16)
        b = (nxt + pe_ref[1:2, :]).astype(BF16)
        h = _dot(a, wa_ref[...]) + _dot(b, wb_ref[...]) + b_ref[...]
        out_ref[...] = _dot(_gelu_tanh(h).astype(BF16), w2_ref[...]).astype(BF16)

    one(kc_ref, pek_ref, wka_ref, wkb_ref, bk_ref, wk2_ref, kout_ref)
    one(vc_ref, pev_ref, wva_ref, wvb_ref, bv_ref, wv2_ref, vout_ref)


def _nsa_kernel(q_ref, kvsw_ref, kc_ref, vc_ref, gates_ref, et_ref, overlap_t_ref, wbias_ref, cbias_ref, o_ref,
                qaug_scr, m_scr, acc_scr, part_scr, sa_scr, sb_scr, *, seq):
    tq = TQ_NSA
    rows = 8 * tq
    n_blk = seq // SEL_BLOCK
    i = pl.program_id(1)
    s0 = i * tq
    q = q_ref[...]
    lane_q = lax.broadcasted_iota(jnp.int32, (tq, LANES), 1)
    lo = lane_q < HALF
    zero = jnp.zeros((tq, LANES), BF16)
    for half in range(2):
        for c in range(4):
            r = 4 * half + c
            chunk = q[:, c * LANES:(c + 1) * LANES]
            qaug_scr[r * tq:(r + 1) * tq, 0:LANES] = jnp.where(lo if half == 0 else jnp.logical_not(lo), chunk, zero)
    qs = qaug_scr[:, 0:LANES]

    ncmp = kc_ref.shape[0]
    wk = WINDOW + tq
    w0 = pl.multiple_of(jnp.maximum(s0 - WINDOW, 0), tq)
    sc = _dot_nt(qs, kc_ref[...])
    sw = _dot_nt(qs, kvsw_ref[pl.ds(w0, wk), 256:384])
    n_idx = lax.broadcasted_iota(jnp.int32, (rows, ncmp), 1)
    pos_c = s0 + (lax.broadcasted_iota(jnp.int32, (rows, ncmp), 0) & (tq - 1))
    cmask = n_idx * CMP_STRIDE + (CMP_BLOCK - 1) <= pos_c
    sm = jnp.where(cmask, sc, NEG)
    e = jnp.where(cmask, jnp.exp2(sm - jnp.max(sm, axis=-1, keepdims=True)), 0.0)
    p_cmp = e / jnp.maximum(jnp.sum(e, axis=-1, keepdims=True), 1e-30)
    p_blocks = [p_cmp[r * tq:(r + 1) * tq] for r in range(8)]
    o_cmp = _dot(p_cmp.astype(BF16), vc_ref[...])
    wbias = wbias_ref[jnp.minimum(i, WINDOW // tq)]
    swb = jnp.concatenate([sw[r * tq:(r + 1) * tq] + wbias for r in range(8)], axis=0)
    e_win = jnp.exp2((swb - jnp.max(swb, axis=-1, keepdims=True)).astype(BF16))
    vw = kvsw_ref[pl.ds(w0, wk), 384:512]
    vw_lane = lax.broadcasted_iota(jnp.int32, vw.shape, 1)
    vw_one = jnp.ones(vw.shape, BF16)
    o_win = jnp.concatenate([_dot(e_win[:rows // 2], jnp.where(vw_lane < HALF, vw, vw_one)),
                             _dot(e_win[rows // 2:], jnp.where(vw_lane < HALF, vw_one, vw))], axis=0)
    o_win = o_win / pltpu.roll(o_win, HALF, 1)
    gates = gates_ref[...]
    for r in range(8):
        sl = slice(r * tq, (r + 1) * tq)
        part_scr[sl] = gates[:, r:r + 1] * o_cmp[sl] + gates[:, 16 + r:17 + r] * o_win[sl]

    blk = lax.broadcasted_iota(jnp.int32, (n_blk, LANES), 0)
    pad = jnp.zeros((LANES - n_blk, LANES), F32)
    imps = []
    for g in range(2):
        psum = (p_blocks[4 * g] + p_blocks[4 * g + 1]) + (p_blocks[4 * g + 2] + p_blocks[4 * g + 3])
        hi = psum.astype(BF16)
        lo_part = (psum - hi.astype(F32)).astype(BF16)
        imps.append(_dot_nt(overlap_t_ref[...], hi) + _dot_nt(overlap_t_ref[...], lo_part))
    for g, part in [(g, part) for g in range(2) for part in range(tq // LANES)]:
        cur = (s0 + part * LANES + lax.broadcasted_iota(jnp.int32, (n_blk, LANES), 1)) // SEL_BLOCK
        valid = blk <= cur
        forced = (blk == 0) | (blk == cur) | (blk == cur - 1)
        imp = imps[g][:, part * LANES:(part + 1) * LANES]
        impv = jnp.where(valid, jnp.where(forced, BIG, imp), -1.0)
        n_slab = n_blk // 8
        slabs = [impv[8 * v:8 * (v + 1)] for v in range(n_slab)]
        cnts = [jnp.zeros((8, LANES), F32) for _ in range(n_slab)]
        for b in range(n_blk):
            other = impv[b:b + 1, :]
            for v in range(n_slab):
                if b < 8 * v:
                    beats = other >= slabs[v]
                elif b >= 8 * (v + 1):
                    beats = other > slabs[v]
                else:
                    beats = (other > slabs[v]) | ((other == slabs[v]) & (blk[0:8] > b - 8 * v))
                cnts[v] = cnts[v] + jnp.where(beats, 1.0, 0.0)
        cnt = jnp.concatenate(cnts, axis=0)
        bias_t = jnp.where(valid & (cnt < float(SEL_TOPK)), 0.0, NEG)
        bias = jnp.concatenate([bias_t, pad], axis=0).T.astype(BF16)
        for r in range(4 * g, 4 * g + 4):
            qaug_scr[r * tq + part * LANES:r * tq + (part + 1) * LANES, LANES:2 * LANES] = bias

    m_scr[...] = jnp.full((rows, LANES), NEG, F32)
    acc_scr[...] = jnp.zeros((rows, LANES), F32)

    def sel_scores(kt, buf):
        k0 = pl.multiple_of(kt * TK_SEL, TK_SEL)
        k_aug = jnp.concatenate([kvsw_ref[pl.ds(k0, TK_SEL), 0:128], et_ref[pl.ds(k0, TK_SEL), :]], axis=1)
        buf[...] = _dot_nt(qaug_scr[...], k_aug)

    def sel_tile(kt, buf, diagonal):
        k0 = pl.multiple_of(kt * TK_SEL, TK_SEL)
        v = kvsw_ref[pl.ds(k0, TK_SEL), 128:256]
        s = buf[...]
        if diagonal:
            cb = cbias_ref[(s0 - k0) // tq]
            s = jnp.concatenate([s[r * tq:(r + 1) * tq] + cb for r in range(8)], axis=0)
        m_prev = m_scr[...]
        m_new = jnp.maximum(m_prev, jnp.max(s, axis=-1, keepdims=True))
        alpha = jnp.exp2(m_prev - m_new)
        e = jnp.exp2((s - jnp.tile(m_new, (1, TK_SEL // LANES))).astype(BF16))
        v_lane = lax.broadcasted_iota(jnp.int32, v.shape, 1)
        one = jnp.ones(v.shape, BF16)
        pv = jnp.concatenate([_dot(e[:rows // 2], jnp.where(v_lane < HALF, v, one)),
                              _dot(e[rows // 2:], jnp.where(v_lane < HALF, one, v))], axis=0)
        acc_scr[...] = alpha * acc_scr[...] + pv
        m_scr[...] = m_new

    last = s0 // TK_SEL
    sel_scores(0, sa_scr)

    def tile_pair(p, carry):
        kt = 2 * p
        sel_scores(kt + 1, sb_scr)
        sel_tile(kt, sa_scr, False)
        sel_scores(kt + 2, sa_scr)
        sel_tile(kt + 1, sb_scr, False)
        return carry

    lax.fori_loop(0, last // 2, tile_pair, 0)
    last_is_odd = (last & 1) == 1

    @pl.when(last_is_odd)
    def _():
        sel_scores(last, sb_scr)
        sel_tile(last - 1, sa_scr, False)
        sel_tile(last, sb_scr, True)

    @pl.when(jnp.logical_not(last_is_odd))
    def _():
        sel_tile(last, sa_scr, True)

    acc = acc_scr[...]
    o_sel = acc / pltpu.roll(acc, HALF, 1)

    gates = gates_ref[...]
    outs = []
    for r in range(8):
        sl = slice(r * tq, (r + 1) * tq)
        outs.append(part_scr[sl] + gates[:, 8 + r:9 + r] * o_sel[sl])
    for c in range(4):
        o_ref[:, c * LANES:(c + 1) * LANES] = jnp.where(lo, outs[c], outs[4 + c]).astype(BF16)


def _mla_kernel(q_ref, k_ref, v_ref, dbias_ref, o_ref, m_scr, acc_scr):
    tq = TQ_MLA
    i = pl.program_id(1)
    s0 = i * tq
    m_scr[...] = jnp.full(m_scr.shape, NEG, F32)
    acc_scr[...] = jnp.zeros(acc_scr.shape, F32)

    def head_lanes(h):
        return slice(h * LANES, (h + 1) * LANES)

    def values(h, k0, nk):
        v2 = v_ref[pl.ds(k0, nk), head_lanes(h // 2)]
        lane = lax.broadcasted_iota(jnp.int32, (nk, LANES), 1)
        one = jnp.ones((nk, LANES), BF16)
        return jnp.where(lane < HALF, v2, one) if h % 2 == 0 else jnp.where(lane < HALF, one, v2)

    def update(h, r0, nr, s, vh):
        m_prev = m_scr[h, r0:r0 + nr]
        m_new = jnp.maximum(m_prev, jnp.max(s, axis=-1, keepdims=True))
        alpha = jnp.exp2(m_prev - m_new)
        e = jnp.exp2((s - jnp.tile(m_new, (1, s.shape[1] // LANES))).astype(BF16))
        acc_scr[h, r0:r0 + nr] = alpha * acc_scr[h, r0:r0 + nr] + _dot(e, vh)
        m_scr[h, r0:r0 + nr] = m_new

    def full_tile(kt, carry):
        k0 = pl.multiple_of(kt * TK_MLA, TK_MLA)
        scores = [_dot_nt(q_ref[:, head_lanes(h)], k_ref[pl.ds(k0, TK_MLA), head_lanes(h)])
                  for h in range(MLA_HEADS)]
        for h in range(MLA_HEADS):
            update(h, 0, tq, scores[h], values(h, k0, TK_MLA))
        return carry

    lax.fori_loop(0, s0 // TK_MLA, full_tile, 0)

    kd = pl.multiple_of(s0, tq)
    dbias = dbias_ref[...]
    scores = [_dot_nt(q_ref[:, head_lanes(h)], k_ref[pl.ds(kd, tq), head_lanes(h)]) for h in range(MLA_HEADS)]
    for h in range(MLA_HEADS):
        update(h, 0, tq, scores[h] + dbias, values(h, kd, tq))
    lo = lax.broadcasted_iota(jnp.int32, (tq, LANES), 1) < HALF
    for c in range(4):
        mixed = jnp.where(lo, acc_scr[2 * c], acc_scr[2 * c + 1])
        denom = jnp.where(lo, acc_scr[2 * c + 1], acc_scr[2 * c])
        o_ref[:, c * LANES:(c + 1) * LANES] = (mixed / pltpu.roll(denom, HALF, 1)).astype(BF16)


def _merge_kernel(x_ref, onsa_ref, omla_ref, wa_ref, wb_ref, wmg_ref, wout_ref, g_ref, b_ref, y_ref):
    n_sub = 4
    sub = x_ref.shape[0] // n_sub
    first = []
    for j in range(n_sub):
        rs = slice(j * sub, (j + 1) * sub)
        first.append((_dot(onsa_ref[rs, :], wa_ref[...]), _dot(omla_ref[rs, :], wb_ref[...]),
                      _dot(x_ref[rs, :].astype(BF16), wmg_ref[...])))
    for j in range(n_sub):
        rs = slice(j * sub, (j + 1) * sub)
        ya, yb, mg_logits = first[j]
        mg = _sigmoid(mg_logits)
        mixed = (mg[:, :D_MODEL] * ya + mg[:, D_MODEL:] * yb).astype(BF16)
        y_ref[rs, :] = _layer_norm(DEEPNORM_ALPHA * x_ref[rs, :] + _dot(mixed, wout_ref[...]), g_ref[...], b_ref[...])


def _ffn_kernel(x_ref, wg_ref, wu_ref, wd_ref, cw_ref, g_ref, b_ref, y_ref, tail_scr, h_scr):
    tm = TM_FFN
    nfc = D_FF // FC_FFN
    first_tile = pl.program_id(1) == 0
    x = x_ref[...]
    xb = x.astype(BF16)
    row = lax.broadcasted_iota(jnp.int32, (tm, FC_FFN), 0)

    @pl.when(first_tile)
    def _():
        tail_scr[...] = jnp.zeros(tail_scr.shape, F32)

    def gate_up(fc):
        cols = slice(fc * FC_FFN, (fc + 1) * FC_FFN)
        return _dot(xb, wg_ref[:, cols]), _dot(xb, wu_ref[:, cols])

    cur = gate_up(0)
    for fc in range(nfc):
        cols = slice(fc * FC_FFN, (fc + 1) * FC_FFN)
        nxt = gate_up(fc + 1) if fc + 1 < nfc else None
        a, up = cur
        tail = tail_scr[:, cols]
        tail_scr[:, cols] = a[tm - 8:, :]
        a1 = jnp.where(row == 0, tail[7:8, :], pltpu.roll(a, 1, 0))
        a2 = jnp.where(row == 0, tail[6:7, :], jnp.where(row == 1, tail[7:8, :], pltpu.roll(a, 2, 0)))
        cw = cw_ref[:, cols]
        conv = cw[0:1, :] * a2 + cw[1:2, :] * a1 + cw[2:3, :] * a + cw[3:4, :]
        h_scr[:, cols] = (_gelu_tanh(conv) * up).astype(BF16)
        cur = nxt
    half = tm // 2
    ys = [_dot(h_scr[j * half:(j + 1) * half, :], wd_ref[...]) for j in range(2)]
    for j in range(2):
        rs = slice(j * half, (j + 1) * half)
        y_ref[rs, :] = _layer_norm(DEEPNORM_ALPHA * x_ref[rs, :] + ys[j], g_ref[...], b_ref[...])


def _rope_tables(seq):
    pos = np.arange(seq, dtype=np.float64)

    def cs(rot):
        inv = ROPE_THETA ** (-np.arange(0, rot, 2, dtype=np.float64) / rot)
        ang = pos[:, None] * inv[None, :]
        return np.cos(ang), np.sin(ang)

    one = lambda n: np.ones((seq, n))
    zero = lambda n: np.zeros((seq, n))
    c8, s8 = cs(NSA_ROT_DIM)
    c16, s16 = cs(MLA_ROPE_DIM)
    nsa_c = np.concatenate([c8, c8, one(48)] * 2, axis=1)
    nsa_s = np.concatenate([-s8, s8, zero(48)] * 2, axis=1)
    mq_c = np.concatenate([one(64), c16, c16, one(32)], axis=1)
    mq_s = np.concatenate([zero(64), -s16, s16, zero(32)], axis=1)
    mk_c = np.concatenate([c16, c16, one(96)], axis=1)
    mk_s = np.concatenate([-s16, s16, zero(96)], axis=1)
    return jnp.asarray(np.stack([nsa_c, nsa_s, mq_c, mq_s, mk_c, mk_s]).astype(np.float32))


def _head_perm():
    return np.array([(c + 4 * half) * 64 + d for c in range(4) for half in range(2) for d in range(64)])


def _window_bias(tq):
    wk = WINDOW + tq
    out = np.zeros((WINDOW // tq + 1, tq, wk), np.float32)
    for v in range(WINDOW // tq + 1):
        s0 = v * tq
        w0 = max(s0 - WINDOW, 0)
        pos = s0 + np.arange(tq)[:, None]
        kpos = w0 + np.arange(wk)[None, :]
        out[v] = np.where((kpos <= pos) & (pos - kpos < WINDOW), 0.0, NEG)
    return jnp.asarray(out)


def _const(shape):
    nd = len(shape)
    return pl.BlockSpec(shape, lambda *_: (0,) * nd, pipeline_mode=pl.Buffered(1))


def _params(n_axes):
    return pltpu.CompilerParams(dimension_semantics=("arbitrary",) * n_axes, vmem_limit_bytes=VMEM_LIMIT)


def kernel(x, w_in, cmp_pe_k, cmp_pe_v, cmp_k_w1, cmp_k_b1, cmp_k_w2, cmp_v_w1, cmp_v_b1, cmp_v_w2,
           nsa_w_o, mla_q_norm, mla_w_uq, mla_kv_norm, mla_w_uk, mla_w_uv, mla_w_o, w_out,
           ln1_g, ln1_b, ffn_w_gate, ffn_w_up, ffn_conv_w, ffn_conv_b, ffn_w_down, ln2_g, ln2_b):
    b, s, d = x.shape
    assert d == D_MODEL and s % TK_SEL == 0 and s % TM_FFN == 0 and (s // CMP_STRIDE) % 8 == 0
    assert WINDOW % TQ_NSA == 0 and s >= WINDOW + TQ_NSA
    t = b * s
    perm = _head_perm()
    xf = x.reshape(t, d)
    for l in range(DEPTH):
        xf = _layer(xf, b, s, perm, w_in[l], cmp_pe_k[l], cmp_pe_v[l], cmp_k_w1[l], cmp_k_b1[l], cmp_k_w2[l],
                    cmp_v_w1[l], cmp_v_b1[l], cmp_v_w2[l], nsa_w_o[l], mla_q_norm[l], mla_w_uq[l],
                    mla_kv_norm[l], mla_w_uk[l], mla_w_uv[l], mla_w_o[l], w_out[l], ln1_g[l], ln1_b[l],
                    ffn_w_gate[l], ffn_w_up[l], ffn_conv_w[l], ffn_conv_b[l], ffn_w_down[l], ln2_g[l], ln2_b[l])
    return xf.reshape(b, s, d)


def _layer(xf, b, s, perm, w_in, pe_k, pe_v, ck_w1, ck_b1, ck_w2, cv_w1, cv_b1, cv_w2, nsa_w_o, q_norm, w_uq,
           kv_norm, w_uk, w_uv, mla_w_o, w_out, ln1_g, ln1_b, w_gate, w_up, conv_w, conv_b, w_down, ln2_g, ln2_b):
    t = b * s
    d = D_MODEL
    o = np.cumsum([0, 512, 128, 128, 128, 128, 128, 128, 24, MLA_Q_RANK, MLA_KV_RANK, MLA_ROPE_DIM, 2 * d])
    wq = w_in[:, o[0]:o[1]][:, perm].astype(BF16)
    wkv = w_in[:, o[1]:o[7]].astype(BF16)
    gate_cols = np.array([h * 3 + br for br in range(3) for h in range(8)])
    wg = jnp.pad(w_in[:, o[7]:o[8]][:, gate_cols], ((0, 0), (0, LANES - 24))).astype(BF16)
    wcq = w_in[:, o[8]:o[9]].astype(BF16)
    wckv = w_in[:, o[9]:o[10]].astype(BF16)
    wkr = jnp.pad(w_in[:, o[10]:o[11]], ((0, 0), (0, LANES - MLA_ROPE_DIM))).astype(BF16)
    wmg = w_in[:, o[11]:o[12]].astype(BF16)

    wuq = jnp.pad(w_uq.reshape(MLA_Q_RANK, MLA_HEADS, MLA_QK_DIM), ((0, 0), (0, 0), (0, LANES - MLA_QK_DIM)))
    wuq = wuq.reshape(MLA_Q_RANK, MLA_HEADS * LANES).astype(BF16)
    wuk = jnp.pad(w_uk.reshape(MLA_KV_RANK, MLA_HEADS, MLA_NOPE_DIM), ((0, 0), (0, 0), (0, LANES - MLA_NOPE_DIM)))
    wuk = wuk.reshape(MLA_KV_RANK, MLA_HEADS * LANES).astype(BF16)
    place = np.zeros((LANES, MLA_HEADS * LANES), np.float32)
    for h in range(MLA_HEADS):
        for r in range(MLA_ROPE_DIM):
            place[r, h * LANES + MLA_NOPE_DIM + r] = 1.0
    place = jnp.asarray(place, BF16)
    tabs = _rope_tables(s)

    tm = TM_PROJ
    n_s = s // tm
    row_blk = lambda w: pl.BlockSpec((tm, w), lambda i: (i, 0))
    outs = pl.pallas_call(
        _proj_kernel,
        grid=(t // tm,),
        in_specs=[row_blk(d), pl.BlockSpec((6, tm, LANES), lambda i: (0, i % n_s, 0)),
                  _const(wq.shape), _const(wkv.shape), _const(wg.shape), _const(wcq.shape), _const(wckv.shape),
                  _const(wkr.shape), _const((1, MLA_Q_RANK)), _const((1, MLA_KV_RANK)), _const(wuq.shape),
                  _const(wuk.shape), _const((MLA_KV_RANK, 512)), _const(place.shape)],
        out_specs=[row_blk(512), row_blk(128), row_blk(128), row_blk(512), row_blk(128),
                   row_blk(1024), row_blk(1024), row_blk(512)],
        out_shape=[jax.ShapeDtypeStruct((t, 512), BF16), jax.ShapeDtypeStruct((t, 128), F32),
                   jax.ShapeDtypeStruct((t, 128), F32), jax.ShapeDtypeStruct((t, 512), BF16),
                   jax.ShapeDtypeStruct((t, 128), F32), jax.ShapeDtypeStruct((t, 1024), BF16),
                   jax.ShapeDtypeStruct((t, 1024), BF16), jax.ShapeDtypeStruct((t, 512), BF16)],
        compiler_params=_params(1),
        name="proj",
    )(xf, tabs, wq, wkv, wg, wcq, wckv, wkr, q_norm.reshape(1, -1), kv_norm.reshape(1, -1), wuq, wuk,
      w_uv.astype(BF16), place)
    qnsa, kcmp, vcmp, kvsw, gates, qmla, kmla, vmla = outs

    nch = s // CMP_STRIDE
    flat = CMP_STRIDE * LANES

    def big_w1(w1):
        w = w1.reshape(2, CMP_STRIDE, NSA_HEAD_DIM, CMP_HIDDEN)
        eye = jnp.eye(2, dtype=F32)
        return jnp.einsum('hldj,ge->hlgdej', w, eye).reshape(2, flat, 2 * CMP_HIDDEN).astype(BF16)

    def big_pe(pe):
        return jnp.tile(pe.reshape(2, CMP_STRIDE, 1, NSA_HEAD_DIM), (1, 1, 2, 1)).reshape(2, flat)

    def big_w2(w2):
        return jnp.einsum('jd,ge->gjed', w2, jnp.eye(2, dtype=F32)).reshape(2 * CMP_HIDDEN, LANES).astype(BF16)

    wk1, wv1 = big_w1(ck_w1), big_w1(cv_w1)
    bk = jnp.tile(ck_b1, 2).reshape(1, -1)
    bv = jnp.tile(cv_b1, 2).reshape(1, -1)
    tok_blk = pl.BlockSpec((s, LANES), lambda i: (i, 0))
    cmp_blk = pl.BlockSpec((None, nch, LANES), lambda i: (i, 0, 0))
    kc, vc = pl.pallas_call(
        _compress_kernel,
        grid=(b,),
        in_specs=[tok_blk, tok_blk, _const((2, flat)), _const((2, flat)),
                  _const((flat, 512)), _const((flat, 512)), _const((flat, 512)), _const((flat, 512)),
                  _const((1, 512)), _const((1, 512)), _const((512, LANES)), _const((512, LANES))],
        out_specs=[cmp_blk, cmp_blk],
        out_shape=[jax.ShapeDtypeStruct((b, nch, LANES), BF16)] * 2,
        compiler_params=_params(1),
        name="compress",
    )(kcmp, vcmp, big_pe(pe_k), big_pe(pe_v),
      wk1[0], wk1[1], wv1[0], wv1[1], bk, bv, big_w2(ck_w2), big_w2(cv_w2))

    n_sel = s // SEL_BLOCK
    key_blk = np.arange(s) // SEL_BLOCK
    onehot_t = jnp.asarray((key_blk[:, None] == np.arange(LANES)[None, :]).astype(np.float32), BF16)
    n_i = np.arange(nch)[None, :]
    j_i = np.arange(n_sel)[:, None]
    overlap_t = ((n_i <= 4 * j_i + 3) & (n_i >= 4 * j_i - 1) & (n_i < nch - 1)).astype(np.float32)
    overlap_t = jnp.asarray(overlap_t, BF16)
    tq = TQ_NSA
    nq = s // tq
    wbias = _window_bias(tq)
    r_i = np.arange(tq)[None, :, None]
    c_i = np.arange(TK_SEL)[None, None, :]
    v_i = np.arange(TK_SEL // tq)[:, None, None]
    cbias = jnp.asarray(np.where(c_i <= v_i * tq + r_i, 0.0, NEG).astype(np.float32))
    onsa = pl.pallas_call(
        functools.partial(_nsa_kernel, seq=s),
        grid=(b, nq),
        in_specs=[pl.BlockSpec((tq, 512), lambda bi, i: (bi * nq + i, 0)),
                  pl.BlockSpec((s, 512), lambda bi, i: (bi, 0)),
                  pl.BlockSpec((None, nch, LANES), lambda bi, i: (bi, 0, 0)),
                  pl.BlockSpec((None, nch, LANES), lambda bi, i: (bi, 0, 0)),
                  pl.BlockSpec((tq, LANES), lambda bi, i: (bi * nq + i, 0)),
                  _const(onehot_t.shape), _const(overlap_t.shape), _const(wbias.shape), _const(cbias.shape)],
        out_specs=pl.BlockSpec((tq, 512), lambda bi, i: (bi * nq + i, 0)),
        out_shape=jax.ShapeDtypeStruct((t, 512), BF16),
        scratch_shapes=[pltpu.VMEM((8 * tq, 2 * LANES), BF16), pltpu.VMEM((8 * tq, LANES), F32),
                        pltpu.VMEM((8 * tq, LANES), F32),
                        pltpu.VMEM((8 * tq, LANES), F32), pltpu.VMEM((8 * tq, TK_SEL), F32),
                        pltpu.VMEM((8 * tq, TK_SEL), F32)],
        compiler_params=_params(2),
        name="nsa",
    )(qnsa, kvsw, kc, vc, gates, onehot_t, overlap_t, wbias, cbias)

    tq = TQ_MLA
    nq = s // tq
    assert tq == TK_MLA
    dbias = jnp.asarray(np.where(np.arange(tq)[None, :] <= np.arange(tq)[:, None], 0.0, NEG)
                        .astype(np.float32))
    omla = pl.pallas_call(
        _mla_kernel,
        grid=(b, nq),
        in_specs=[pl.BlockSpec((tq, 1024), lambda bi, i: (bi * nq + i, 0)),
                  pl.BlockSpec((s, 1024), lambda bi, i: (bi, 0)),
                  pl.BlockSpec((s, 512), lambda bi, i: (bi, 0)), _const(dbias.shape)],
        out_specs=pl.BlockSpec((tq, 512), lambda bi, i: (bi * nq + i, 0)),
        out_shape=jax.ShapeDtypeStruct((t, 512), BF16),
        scratch_shapes=[pltpu.VMEM((MLA_HEADS, tq, LANES), F32), pltpu.VMEM((MLA_HEADS, tq, LANES), F32)],
        compiler_params=_params(2),
        name="mla",
    )(qmla, kmla, vmla, dbias)

    tm = TM_MERGE
    row_blk = lambda w: pl.BlockSpec((tm, w), lambda i: (i, 0))
    x1 = pl.pallas_call(
        _merge_kernel,
        grid=(t // tm,),
        in_specs=[row_blk(d), row_blk(512), row_blk(512), _const((512, d)), _const((512, d)),
                  _const((d, 2 * d)), _const((d, d)), _const((1, d)), _const((1, d))],
        out_specs=row_blk(d),
        out_shape=jax.ShapeDtypeStruct((t, d), F32),
        compiler_params=_params(1),
        name="merge",
    )(xf, onsa, omla, nsa_w_o[perm, :].astype(BF16), mla_w_o.astype(BF16), wmg, w_out.astype(BF16),
      ln1_g.reshape(1, -1), ln1_b.reshape(1, -1))

    tm = TM_FFN
    nfc = D_FF // FC_FFN
    n_s = s // tm
    conv_tab = jnp.concatenate([conv_w, conv_b.reshape(1, -1), jnp.zeros((4, D_FF), F32)], axis=0)
    blk = pl.BlockSpec((tm, d), lambda bi, i: (bi * n_s + i, 0))
    y = pl.pallas_call(
        _ffn_kernel,
        grid=(b, n_s),
        in_specs=[blk, _const((d, D_FF)), _const((d, D_FF)), _const((D_FF, d)), _const((8, D_FF)),
                  _const((1, d)), _const((1, d))],
        out_specs=blk,
        out_shape=jax.ShapeDtypeStruct((t, d), F32),
        scratch_shapes=[pltpu.VMEM((8, D_FF), F32), pltpu.VMEM((tm, D_FF), BF16)],
        compiler_params=_params(2),
        name="ffn",
    )(x1, w_gate.astype(BF16), w_up.astype(BF16), w_down.astype(BF16), conv_tab,
      ln2_g.reshape(1, -1), ln2_b.reshape(1, -1))
    return y
```

```python
import functools

import numpy as np
import jax
import jax.numpy as jnp
from jax import lax
from jax.experimental import pallas as pl
from jax.experimental.pallas import tpu as pltpu

D_MODEL = 1024
ROPE_THETA = 500000.0
NSA_HEADS = 8
NSA_KV_GROUPS = 2
NSA_GROUP = 4
NSA_HEAD_DIM = 64
NSA_ROT_DIM = 16
CMP_BLOCK = 32
CMP_STRIDE = 16
CMP_HIDDEN = 256
SEL_BLOCK = 64
SEL_TOPK = 8
WINDOW = 256
MLA_HEADS = 8
MLA_Q_RANK = 768
MLA_KV_RANK = 256
MLA_NOPE_DIM = 64
MLA_ROPE_DIM = 32
MLA_V_DIM = 64
MLA_QK_DIM = 96
D_FF = 2816
LN_EPS = 1e-5
RMS_EPS = 1e-6
DEPTH = 1
DEEPNORM_ALPHA = (2 * DEPTH) ** 0.25
LOG2E = 1.4426950408889634

LANES = 128
HALF = LANES // 2
NEG = -1e30
BIG = 3e38
VMEM_LIMIT = 56 * 1024 * 1024

TM_PROJ = 1024
TQ_NSA = 256
TK_SEL = 512
TQ_MLA = 512
TK_MLA = 512
TM_MERGE = 1024
TM_FFN = 1024
FC_FFN = 256

BF16 = jnp.bfloat16
F32 = jnp.float32


def _dot(a, b):
    return jnp.dot(a, b, preferred_element_type=F32)


def _dot_nt(a, b):
    return lax.dot_general(a, b, (((1,), (1,)), ((), ())), preferred_element_type=F32)


def _gelu_tanh(x):
    return x * (0.5 * (1.0 + jnp.tanh(0.7978845608028654 * (x + 0.044715 * (x * x * x)))))


def _sigmoid(x):
    return 1.0 / (1.0 + jnp.exp(-x))


def _layer_norm(x, g, b):
    mu = jnp.mean(x, axis=-1, keepdims=True)
    xc = x - mu
    var = jnp.mean(xc * xc, axis=-1, keepdims=True)
    return xc * lax.rsqrt(var + LN_EPS) * g + b


def _rms_norm(x, g):
    return x * lax.rsqrt(jnp.mean(x * x, axis=-1, keepdims=True) + RMS_EPS) * g


def _rope_chunk(x, cos, sin_signed, first_half, half):
    partner = jnp.where(first_half, pltpu.roll(x, LANES - half, 1), pltpu.roll(x, half, 1))
    return x * cos + partner * sin_signed


def _softmax_rows(s):
    e = jnp.exp2(s - jnp.max(s, axis=-1, keepdims=True))
    return e / jnp.sum(e, axis=-1, keepdims=True)


def _proj_kernel(x_ref, tab_ref, wq_ref, wkv_ref, wg_ref, wcq_ref, wckv_ref, wkr_ref,
                 qn_ref, kvn_ref, wuq_ref, wuk_ref, wuv_ref, place_ref,
                 qnsa_ref, kcmp_ref, vcmp_ref, kvsw_ref, gates_ref, qmla_ref, kmla_ref, vmla_ref):
    xb = x_ref[...].astype(BF16)
    tm = xb.shape[0]
    lane = lax.broadcasted_iota(jnp.int32, (tm, LANES), 1)
    nsa_first = (lane % HALF) < (NSA_ROT_DIM // 2)
    mlaq_first = lane < (MLA_NOPE_DIM + MLA_ROPE_DIM // 2)
    mlak_first = lane < (MLA_ROPE_DIM // 2)
    cn, sn = tab_ref[0], tab_ref[1]
    cq_t, sq_t = tab_ref[2], tab_ref[3]
    ck_t, sk_t = tab_ref[4], tab_ref[5]

    def rope_n(v):
        return _rope_chunk(v, cn, sn, nsa_first, NSA_ROT_DIM // 2)

    q = _dot(xb, wq_ref[...])
    kv = _dot(xb, wkv_ref[...])
    cq_raw = _dot(xb, wcq_ref[...])
    ckv_raw = _dot(xb, wckv_ref[...])
    kr_raw = _dot(xb, wkr_ref[...])
    gate_logits = _dot(xb, wg_ref[...])

    cq = _rms_norm(cq_raw, qn_ref[...]).astype(BF16)
    qm = _dot(cq, wuq_ref[...])
    ckv = _rms_norm(ckv_raw, kvn_ref[...]).astype(BF16)
    kr = _rope_chunk(kr_raw, ck_t, sk_t, mlak_first, MLA_ROPE_DIM // 2).astype(BF16)
    kmla_ref[...] = (_dot(ckv, wuk_ref[...]) + _dot(kr, place_ref[...])).astype(BF16)
    vmla_ref[...] = _dot(ckv, wuv_ref[...]).astype(BF16)

    for c in range(4):
        sl = slice(c * LANES, (c + 1) * LANES)
        qnsa_ref[:, sl] = (rope_n(q[:, sl]) * (NSA_HEAD_DIM ** -0.5 * LOG2E)).astype(BF16)

    kcmp_ref[...] = rope_n(kv[:, 0:128])
    vcmp_ref[...] = kv[:, 128:256]
    kvsw_ref[:, 0:128] = rope_n(kv[:, 256:384]).astype(BF16)
    kvsw_ref[:, 128:256] = kv[:, 384:512].astype(BF16)
    kvsw_ref[:, 256:384] = rope_n(kv[:, 512:640]).astype(BF16)
    kvsw_ref[:, 384:512] = kv[:, 640:768].astype(BF16)

    gates_ref[...] = _sigmoid(gate_logits)

    for h in range(MLA_HEADS):
        sl = slice(h * LANES, (h + 1) * LANES)
        roped = _rope_chunk(qm[:, sl], cq_t, sq_t, mlaq_first, MLA_ROPE_DIM // 2)
        qmla_ref[:, sl] = (roped * (MLA_QK_DIM ** -0.5 * LOG2E)).astype(BF16)


def _compress_kernel(kc_ref, vc_ref, pek_ref, pev_ref, wk1_ref, wv1_ref, bk_ref, bv_ref, wk2_ref, wv2_ref,
                     kout_ref, vout_ref):
    def one(tok_ref, pe_ref, w1_ref, b_ref, w2_ref, out_ref):
        nch = tok_ref.shape[0] // CMP_STRIDE
        ch = jnp.concatenate([tok_ref[pl.ds(l, nch, stride=CMP_STRIDE), :] for l in range(CMP_STRIDE)], axis=1)
        nxt = pltpu.roll(ch, nch - 1, 0)
        a = ch + pe_ref[0:1, :]
        b = nxt + pe_ref[1:2, :]
        first_group = (lax.broadcasted_iota(jnp.int32, a.shape, 1) % LANES) < HALF
        out = None
        for g in range(NSA_KV_GROUPS):
            keep = first_group if g == 0 else jnp.logical_not(first_group)
            ag = jnp.where(keep, a, 0.0).astype(BF16)
            bg = jnp.where(keep, b, 0.0).astype(BF16)
            h = _dot(ag, w1_ref[0]) + _dot(bg, w1_ref[1]) + b_ref[...]
            og = _dot(_gelu_tanh(h).astype(BF16), w2_ref[g])
            out = og if out is None else out + og
        out_ref[...] = out.astype(BF16)

    one(kc_ref, pek_ref, wk1_ref, bk_ref, wk2_ref, kout_ref)
    one(vc_ref, pev_ref, wv1_ref, bv_ref, wv2_ref, vout_ref)


def _nsa_kernel(q_ref, kvsw_ref, kc_ref, vc_ref, gates_ref, et_ref, overlap_t_ref, wbias_ref, cbias_ref, o_ref,
                qaug_scr, m_scr, acc_scr, part_scr, sa_scr, sb_scr, *, seq):
    tq = TQ_NSA
    rows = 8 * tq
    n_blk = seq // SEL_BLOCK
    i = pl.program_id(1)
    s0 = i * tq
    q = q_ref[...]
    lane_q = lax.broadcasted_iota(jnp.int32, (tq, LANES), 1)
    lo = lane_q < HALF
    zero = jnp.zeros((tq, LANES), BF16)
    for half in range(2):
        for c in range(4):
            r = 4 * half + c
            chunk = q[:, c * LANES:(c + 1) * LANES]
            qaug_scr[r * tq:(r + 1) * tq, 0:LANES] = jnp.where(lo if half == 0 else jnp.logical_not(lo), chunk, zero)
    qs = qaug_scr[:, 0:LANES]

    ncmp = kc_ref.shape[0]
    wk = WINDOW + tq
    w0 = pl.multiple_of(jnp.maximum(s0 - WINDOW, 0), tq)
    sc = _dot_nt(qs, kc_ref[...])
    sw = _dot_nt(qs, kvsw_ref[pl.ds(w0, wk), 256:384])
    n_idx = lax.broadcasted_iota(jnp.int32, (rows, ncmp), 1)
    pos_c = s0 + (lax.broadcasted_iota(jnp.int32, (rows, ncmp), 0) & (tq - 1))
    cmask = n_idx * CMP_STRIDE + (CMP_BLOCK - 1) <= pos_c
    sm = jnp.where(cmask, sc, NEG)
    e = jnp.where(cmask, jnp.exp2(sm - jnp.max(sm, axis=-1, keepdims=True)), 0.0)
    p_cmp = e / jnp.maximum(jnp.sum(e, axis=-1, keepdims=True), 1e-30)
    p_blocks = [p_cmp[r * tq:(r + 1) * tq] for r in range(8)]
    o_cmp = _dot(p_cmp.astype(BF16), vc_ref[...])
    wbias = wbias_ref[jnp.minimum(i, WINDOW // tq)]
    swb = jnp.concatenate([sw[r * tq:(r + 1) * tq] + wbias for r in range(8)], axis=0)
    e_win = jnp.exp2((swb - jnp.max(swb, axis=-1, keepdims=True)).astype(BF16))
    vw = kvsw_ref[pl.ds(w0, wk), 384:512]
    vw_lane = lax.broadcasted_iota(jnp.int32, vw.shape, 1)
    vw_one = jnp.ones(vw.shape, BF16)
    o_win = jnp.concatenate([_dot(e_win[:rows // 2], jnp.where(vw_lane < HALF, vw, vw_one)),
                             _dot(e_win[rows // 2:], jnp.where(vw_lane < HALF, vw_one, vw))], axis=0)
    gates = gates_ref[...]

    def head_rows(a, r):
        return a[r * tq:(r + 1) * tq]

    def merged(a, c):
        return jnp.where(lo, head_rows(a, c), head_rows(a, 4 + c))

    def merged_denominator(a, c):
        return pltpu.roll(jnp.where(lo, head_rows(a, 4 + c), head_rows(a, c)), HALF, 1)

    def gate(branch, c):
        return jnp.where(lo, gates[:, 8 * branch + c:8 * branch + c + 1],
                         gates[:, 8 * branch + 4 + c:8 * branch + 5 + c])

    for c in range(4):
        part_scr[c * tq:(c + 1) * tq] = (gate(0, c) * merged(o_cmp, c)
                                         + gate(2, c) * (merged(o_win, c) / merged_denominator(o_win, c)))

    blk = lax.broadcasted_iota(jnp.int32, (n_blk, LANES), 0)
    pad = jnp.zeros((LANES - n_blk, LANES), F32)
    imps = []
    for g in range(2):
        psum = (p_blocks[4 * g] + p_blocks[4 * g + 1]) + (p_blocks[4 * g + 2] + p_blocks[4 * g + 3])
        hi = psum.astype(BF16)
        lo_part = (psum - hi.astype(F32)).astype(BF16)
        imps.append(_dot_nt(overlap_t_ref[...], hi) + _dot_nt(overlap_t_ref[...], lo_part))
    for g, part in [(g, part) for g in range(2) for part in range(tq // LANES)]:
        cur = (s0 + part * LANES + lax.broadcasted_iota(jnp.int32, (n_blk, LANES), 1)) // SEL_BLOCK
        valid = blk <= cur
        forced = (blk == 0) | (blk == cur) | (blk == cur - 1)
        imp = imps[g][:, part * LANES:(part + 1) * LANES]
        impv = jnp.where(valid, jnp.where(forced, BIG, imp), -1.0)
        n_slab = n_blk // 8
        slabs = [impv[8 * v:8 * (v + 1)] for v in range(n_slab)]
        cnts = [jnp.zeros((8, LANES), F32) for _ in range(n_slab)]
        for b in range(n_blk):
            other = impv[b:b + 1, :]
            for v in range(n_slab):
                if b < 8 * v:
                    beats = other >= slabs[v]
                elif b >= 8 * (v + 1):
                    beats = other > slabs[v]
                else:
                    beats = (other > slabs[v]) | ((other == slabs[v]) & (blk[0:8] > b - 8 * v))
                cnts[v] = cnts[v] + jnp.where(beats, 1.0, 0.0)
        cnt = jnp.concatenate(cnts, axis=0)
        bias_t = jnp.where(valid & (cnt < float(SEL_TOPK)), 0.0, NEG)
        bias = jnp.concatenate([bias_t, pad], axis=0).T.astype(BF16)
        for r in range(4 * g, 4 * g + 4):
            qaug_scr[r * tq + part * LANES:r * tq + (part + 1) * LANES, LANES:2 * LANES] = bias

    m_scr[...] = jnp.full((rows, LANES), NEG, F32)
    acc_scr[...] = jnp.zeros((rows, LANES), F32)

    def sel_scores(kt, buf):
        k0 = pl.multiple_of(kt * TK_SEL, TK_SEL)
        k_aug = jnp.concatenate([kvsw_ref[pl.ds(k0, TK_SEL), 0:128], et_ref[pl.ds(k0, TK_SEL), :]], axis=1)
        buf[...] = _dot_nt(qaug_scr[...], k_aug)

    def sel_tile(kt, buf, diagonal):
        k0 = pl.multiple_of(kt * TK_SEL, TK_SEL)
        v = kvsw_ref[pl.ds(k0, TK_SEL), 128:256]
        s = buf[...]
        if diagonal:
            cb = cbias_ref[(s0 - k0) // tq]
            s = jnp.concatenate([s[r * tq:(r + 1) * tq] + cb for r in range(8)], axis=0)
        m_prev = m_scr[...]
        m_new = jnp.maximum(m_prev, jnp.max(s, axis=-1, keepdims=True))
        alpha = jnp.exp2(m_prev - m_new)
        e = jnp.exp2((s - jnp.tile(m_new, (1, TK_SEL // LANES))).astype(BF16))
        v_lane = lax.broadcasted_iota(jnp.int32, v.shape, 1)
        one = jnp.ones(v.shape, BF16)
        pv = jnp.concatenate([_dot(e[:rows // 2], jnp.where(v_lane < HALF, v, one)),
                              _dot(e[rows // 2:], jnp.where(v_lane < HALF, one, v))], axis=0)
        acc_scr[...] = alpha * acc_scr[...] + pv
        m_scr[...] = m_new

    last = s0 // TK_SEL
    sel_scores(0, sa_scr)

    def tile_pair(p, carry):
        kt = 2 * p
        sel_scores(kt + 1, sb_scr)
        sel_tile(kt, sa_scr, False)
        sel_scores(kt + 2, sa_scr)
        sel_tile(kt + 1, sb_scr, False)
        return carry

    lax.fori_loop(0, last // 2, tile_pair, 0)
    last_is_odd = (last & 1) == 1

    @pl.when(last_is_odd)
    def _():
        sel_scores(last, sb_scr)
        sel_tile(last - 1, sa_scr, False)
        sel_tile(last, sb_scr, True)

    @pl.when(jnp.logical_not(last_is_odd))
    def _():
        sel_tile(last, sa_scr, True)

    acc = acc_scr[...]
    for c in range(4):
        o_sel = merged(acc, c) / merged_denominator(acc, c)
        o_ref[:, c * LANES:(c + 1) * LANES] = (part_scr[c * tq:(c + 1) * tq] + gate(1, c) * o_sel).astype(BF16)


def _mla_kernel(q_ref, k_ref, v_ref, dbias_ref, o_ref, m_scr, acc_scr):
    tq = TQ_MLA
    i = pl.program_id(1)
    s0 = i * tq
    m_scr[...] = jnp.full(m_scr.shape, NEG, F32)
    acc_scr[...] = jnp.zeros(acc_scr.shape, F32)

    def head_lanes(h):
        return slice(h * LANES, (h + 1) * LANES)

    def values(h, k0, nk):
        v2 = v_ref[pl.ds(k0, nk), head_lanes(h // 2)]
        lane = lax.broadcasted_iota(jnp.int32, (nk, LANES), 1)
        one = jnp.ones((nk, LANES), BF16)
        return jnp.where(lane < HALF, v2, one) if h % 2 == 0 else jnp.where(lane < HALF, one, v2)

    def update(h, r0, nr, s, vh):
        m_prev = m_scr[h, r0:r0 + nr]
        m_new = jnp.maximum(m_prev, jnp.max(s, axis=-1, keepdims=True))
        alpha = jnp.exp2(m_prev - m_new)
        e = jnp.exp2((s - jnp.tile(m_new, (1, s.shape[1] // LANES))).astype(BF16))
        acc_scr[h, r0:r0 + nr] = alpha * acc_scr[h, r0:r0 + nr] + _dot(e, vh)
        m_scr[h, r0:r0 + nr] = m_new

    def full_tile(kt, carry):
        k0 = pl.multiple_of(kt * TK_MLA, TK_MLA)
        scores = [_dot_nt(q_ref[:, head_lanes(h)], k_ref[pl.ds(k0, TK_MLA), head_lanes(h)])
                  for h in range(MLA_HEADS)]
        for h in range(MLA_HEADS):
            update(h, 0, tq, scores[h], values(h, k0, TK_MLA))
        return carry

    lax.fori_loop(0, s0 // TK_MLA, full_tile, 0)

    kd = pl.multiple_of(s0, tq)
    dbias = dbias_ref[...]
    scores = [_dot_nt(q_ref[:, head_lanes(h)], k_ref[pl.ds(kd, tq), head_lanes(h)]) for h in range(MLA_HEADS)]
    for h in range(MLA_HEADS):
        update(h, 0, tq, scores[h] + dbias, values(h, kd, tq))
    lo = lax.broadcasted_iota(jnp.int32, (tq, LANES), 1) < HALF
    for c in range(4):
        mixed = jnp.where(lo, acc_scr[2 * c], acc_scr[2 * c + 1])
        denom = jnp.where(lo, acc_scr[2 * c + 1], acc_scr[2 * c])
        o_ref[:, c * LANES:(c + 1) * LANES] = (mixed / pltpu.roll(denom, HALF, 1)).astype(BF16)


def _merge_kernel(x_ref, onsa_ref, omla_ref, wa_ref, wb_ref, wmg_ref, wout_ref, g_ref, b_ref, y_ref):
    n_sub = 4
    sub = x_ref.shape[0] // n_sub
    first = []
    for j in range(n_sub):
        rs = slice(j * sub, (j + 1) * sub)
        first.append((_dot(onsa_ref[rs, :], wa_ref[...]), _dot(omla_ref[rs, :], wb_ref[...]),
                      _dot(x_ref[rs, :].astype(BF16), wmg_ref[...])))
    for j in range(n_sub):
        rs = slice(j * sub, (j + 1) * sub)
        ya, yb, mg_logits = first[j]
        mg = _sigmoid(mg_logits)
        mixed = (mg[:, :D_MODEL] * ya + mg[:, D_MODEL:] * yb).astype(BF16)
        y_ref[rs, :] = _layer_norm(DEEPNORM_ALPHA * x_ref[rs, :] + _dot(mixed, wout_ref[...]), g_ref[...], b_ref[...])


def _ffn_kernel(x_ref, wg_ref, wu_ref, wd_ref, cw_ref, g_ref, b_ref, y_ref, tail_scr, h_scr):
    tm = TM_FFN
    nfc = D_FF // FC_FFN
    first_tile = pl.program_id(1) == 0
    x = x_ref[...]
    xb = x.astype(BF16)
    row = lax.broadcasted_iota(jnp.int32, (tm, FC_FFN), 0)

    @pl.when(first_tile)
    def _():
        tail_scr[...] = jnp.zeros(tail_scr.shape, F32)

    def gate_up(fc):
        cols = slice(fc * FC_FFN, (fc + 1) * FC_FFN)
        return _dot(xb, wg_ref[:, cols]), _dot(xb, wu_ref[:, cols])

    cur = gate_up(0)
    for fc in range(nfc):
        cols = slice(fc * FC_FFN, (fc + 1) * FC_FFN)
        nxt = gate_up(fc + 1) if fc + 1 < nfc else None
        a, up = cur
        tail = tail_scr[:, cols]
        tail_scr[:, cols] = a[tm - 8:, :]
        a1 = jnp.where(row == 0, tail[7:8, :], pltpu.roll(a, 1, 0))
        a2 = jnp.where(row == 0, tail[6:7, :], jnp.where(row == 1, tail[7:8, :], pltpu.roll(a, 2, 0)))
        cw = cw_ref[:, cols]
        conv = cw[0:1, :] * a2 + cw[1:2, :] * a1 + cw[2:3, :] * a + cw[3:4, :]
        h_scr[:, cols] = (_gelu_tanh(conv) * up).astype(BF16)
        cur = nxt
    half = tm // 2
    ys = [_dot(h_scr[j * half:(j + 1) * half, :], wd_ref[...]) for j in range(2)]
    for j in range(2):
        rs = slice(j * half, (j + 1) * half)
        y_ref[rs, :] = _layer_norm(DEEPNORM_ALPHA * x_ref[rs, :] + ys[j], g_ref[...], b_ref[...])


def _rope_tables(seq):
    pos = np.arange(seq, dtype=np.float64)

    def cs(rot):
        inv = ROPE_THETA ** (-np.arange(0, rot, 2, dtype=np.float64) / rot)
        ang = pos[:, None] * inv[None, :]
        return np.cos(ang), np.sin(ang)

    one = lambda n: np.ones((seq, n))
    zero = lambda n: np.zeros((seq, n))
    c8, s8 = cs(NSA_ROT_DIM)
    c16, s16 = cs(MLA_ROPE_DIM)
    nsa_c = np.concatenate([c8, c8, one(48)] * 2, axis=1)
    nsa_s = np.concatenate([-s8, s8, zero(48)] * 2, axis=1)
    mq_c = np.concatenate([one(64), c16, c16, one(32)], axis=1)
    mq_s = np.concatenate([zero(64), -s16, s16, zero(32)], axis=1)
    mk_c = np.concatenate([c16, c16, one(96)], axis=1)
    mk_s = np.concatenate([-s16, s16, zero(96)], axis=1)
    return jnp.asarray(np.stack([nsa_c, nsa_s, mq_c, mq_s, mk_c, mk_s]).astype(np.float32))


def _head_perm():
    return np.array([(c + 4 * half) * 64 + d for c in range(4) for half in range(2) for d in range(64)])


def _window_bias(tq):
    wk = WINDOW + tq
    out = np.zeros((WINDOW // tq + 1, tq, wk), np.float32)
    for v in range(WINDOW // tq + 1):
        s0 = v * tq
        w0 = max(s0 - WINDOW, 0)
        pos = s0 + np.arange(tq)[:, None]
        kpos = w0 + np.arange(wk)[None, :]
        out[v] = np.where((kpos <= pos) & (pos - kpos < WINDOW), 0.0, NEG)
    return jnp.asarray(out)


def _const(shape):
    nd = len(shape)
    return pl.BlockSpec(shape, lambda *_: (0,) * nd, pipeline_mode=pl.Buffered(1))


def _params(n_axes):
    return pltpu.CompilerParams(dimension_semantics=("arbitrary",) * n_axes, vmem_limit_bytes=VMEM_LIMIT)


def kernel(x, w_in, cmp_pe_k, cmp_pe_v, cmp_k_w1, cmp_k_b1, cmp_k_w2, cmp_v_w1, cmp_v_b1, cmp_v_w2,
           nsa_w_o, mla_q_norm, mla_w_uq, mla_kv_norm, mla_w_uk, mla_w_uv, mla_w_o, w_out,
           ln1_g, ln1_b, ffn_w_gate, ffn_w_up, ffn_conv_w, ffn_conv_b, ffn_w_down, ln2_g, ln2_b):
    b, s, d = x.shape
    assert d == D_MODEL and s % TK_SEL == 0 and s % TM_FFN == 0 and (s // CMP_STRIDE) % 8 == 0
    assert WINDOW % TQ_NSA == 0 and s >= WINDOW + TQ_NSA
    t = b * s
    perm = _head_perm()
    xf = x.reshape(t, d)
    for l in range(DEPTH):
        xf = _layer(xf, b, s, perm, w_in[l], cmp_pe_k[l], cmp_pe_v[l], cmp_k_w1[l], cmp_k_b1[l], cmp_k_w2[l],
                    cmp_v_w1[l], cmp_v_b1[l], cmp_v_w2[l], nsa_w_o[l], mla_q_norm[l], mla_w_uq[l],
                    mla_kv_norm[l], mla_w_uk[l], mla_w_uv[l], mla_w_o[l], w_out[l], ln1_g[l], ln1_b[l],
                    ffn_w_gate[l], ffn_w_up[l], ffn_conv_w[l], ffn_conv_b[l], ffn_w_down[l], ln2_g[l], ln2_b[l])
    return xf.reshape(b, s, d)


def _layer(xf, b, s, perm, w_in, pe_k, pe_v, ck_w1, ck_b1, ck_w2, cv_w1, cv_b1, cv_w2, nsa_w_o, q_norm, w_uq,
           kv_norm, w_uk, w_uv, mla_w_o, w_out, ln1_g, ln1_b, w_gate, w_up, conv_w, conv_b, w_down, ln2_g, ln2_b):
    t = b * s
    d = D_MODEL
    o = np.cumsum([0, 512, 128, 128, 128, 128, 128, 128, 24, MLA_Q_RANK, MLA_KV_RANK, MLA_ROPE_DIM, 2 * d])
    wq = w_in[:, o[0]:o[1]][:, perm].astype(BF16)
    wkv = w_in[:, o[1]:o[7]].astype(BF16)
    gate_cols = np.array([h * 3 + br for br in range(3) for h in range(8)])
    wg = jnp.pad(w_in[:, o[7]:o[8]][:, gate_cols], ((0, 0), (0, LANES - 24))).astype(BF16)
    wcq = w_in[:, o[8]:o[9]].astype(BF16)
    wckv = w_in[:, o[9]:o[10]].astype(BF16)
    wkr = jnp.pad(w_in[:, o[10]:o[11]], ((0, 0), (0, LANES - MLA_ROPE_DIM))).astype(BF16)
    wmg = w_in[:, o[11]:o[12]].astype(BF16)

    wuq = jnp.pad(w_uq.reshape(MLA_Q_RANK, MLA_HEADS, MLA_QK_DIM), ((0, 0), (0, 0), (0, LANES - MLA_QK_DIM)))
    wuq = wuq.reshape(MLA_Q_RANK, MLA_HEADS * LANES).astype(BF16)
    wuk = jnp.pad(w_uk.reshape(MLA_KV_RANK, MLA_HEADS, MLA_NOPE_DIM), ((0, 0), (0, 0), (0, LANES - MLA_NOPE_DIM)))
    wuk = wuk.reshape(MLA_KV_RANK, MLA_HEADS * LANES).astype(BF16)
    place = np.zeros((LANES, MLA_HEADS * LANES), np.float32)
    for h in range(MLA_HEADS):
        for r in range(MLA_ROPE_DIM):
            place[r, h * LANES + MLA_NOPE_DIM + r] = 1.0
    place = jnp.asarray(place, BF16)
    tabs = _rope_tables(s)

    tm = TM_PROJ
    n_s = s // tm
    row_blk = lambda w: pl.BlockSpec((tm, w), lambda i: (i, 0))
    outs = pl.pallas_call(
        _proj_kernel,
        grid=(t // tm,),
        in_specs=[row_blk(d), pl.BlockSpec((6, tm, LANES), lambda i: (0, i % n_s, 0)),
                  _const(wq.shape), _const(wkv.shape), _const(wg.shape), _const(wcq.shape), _const(wckv.shape),
                  _const(wkr.shape), _const((1, MLA_Q_RANK)), _const((1, MLA_KV_RANK)), _const(wuq.shape),
                  _const(wuk.shape), _const((MLA_KV_RANK, 512)), _const(place.shape)],
        out_specs=[row_blk(512), row_blk(128), row_blk(128), row_blk(512), row_blk(128),
                   row_blk(1024), row_blk(1024), row_blk(512)],
        out_shape=[jax.ShapeDtypeStruct((t, 512), BF16), jax.ShapeDtypeStruct((t, 128), F32),
                   jax.ShapeDtypeStruct((t, 128), F32), jax.ShapeDtypeStruct((t, 512), BF16),
                   jax.ShapeDtypeStruct((t, 128), F32), jax.ShapeDtypeStruct((t, 1024), BF16),
                   jax.ShapeDtypeStruct((t, 1024), BF16), jax.ShapeDtypeStruct((t, 512), BF16)],
        compiler_params=_params(1),
        name="proj",
    )(xf, tabs, wq, wkv, wg, wcq, wckv, wkr, q_norm.reshape(1, -1), kv_norm.reshape(1, -1), wuq, wuk,
      w_uv.astype(BF16), place)
    qnsa, kcmp, vcmp, kvsw, gates, qmla, kmla, vmla = outs

    nch = s // CMP_STRIDE
    flat = CMP_STRIDE * LANES

    def dup_w1(w1):
        w = w1.astype(BF16).reshape(2, CMP_STRIDE, 1, NSA_HEAD_DIM, CMP_HIDDEN)
        return jnp.broadcast_to(w, (2, CMP_STRIDE, 2, NSA_HEAD_DIM, CMP_HIDDEN)).reshape(2, flat, CMP_HIDDEN)

    def big_pe(pe):
        return jnp.tile(pe.reshape(2, CMP_STRIDE, 1, NSA_HEAD_DIM), (1, 1, 2, 1)).reshape(2, flat)

    def placed_w2(w2):
        w = w2.astype(BF16)
        z = jnp.zeros_like(w)
        return jnp.stack([jnp.concatenate([w, z], axis=1), jnp.concatenate([z, w], axis=1)])

    tok_blk = pl.BlockSpec((s, LANES), lambda i: (i, 0))
    cmp_blk = pl.BlockSpec((None, nch, LANES), lambda i: (i, 0, 0))
    kc, vc = pl.pallas_call(
        _compress_kernel,
        grid=(b,),
        in_specs=[tok_blk, tok_blk, _const((2, flat)), _const((2, flat)),
                  _const((2, flat, CMP_HIDDEN)), _const((2, flat, CMP_HIDDEN)),
                  _const((1, CMP_HIDDEN)), _const((1, CMP_HIDDEN)),
                  _const((2, CMP_HIDDEN, LANES)), _const((2, CMP_HIDDEN, LANES))],
        out_specs=[cmp_blk, cmp_blk],
        out_shape=[jax.ShapeDtypeStruct((b, nch, LANES), BF16)] * 2,
        compiler_params=_params(1),
        name="compress",
    )(kcmp, vcmp, big_pe(pe_k), big_pe(pe_v), dup_w1(ck_w1), dup_w1(cv_w1),
      ck_b1.reshape(1, -1), cv_b1.reshape(1, -1), placed_w2(ck_w2), placed_w2(cv_w2))

    n_sel = s // SEL_BLOCK
    key_blk = np.arange(s) // SEL_BLOCK
    onehot_t = jnp.asarray((key_blk[:, None] == np.arange(LANES)[None, :]).astype(np.float32), BF16)
    n_i = np.arange(nch)[None, :]
    j_i = np.arange(n_sel)[:, None]
    overlap_t = ((n_i <= 4 * j_i + 3) & (n_i >= 4 * j_i - 1) & (n_i < nch - 1)).astype(np.float32)
    overlap_t = jnp.asarray(overlap_t, BF16)
    tq = TQ_NSA
    nq = s // tq
    wbias = _window_bias(tq)
    r_i = np.arange(tq)[None, :, None]
    c_i = np.arange(TK_SEL)[None, None, :]
    v_i = np.arange(TK_SEL // tq)[:, None, None]
    cbias = jnp.asarray(np.where(c_i <= v_i * tq + r_i, 0.0, NEG).astype(np.float32))
    onsa = pl.pallas_call(
        functools.partial(_nsa_kernel, seq=s),
        grid=(b, nq),
        in_specs=[pl.BlockSpec((tq, 512), lambda bi, i: (bi * nq + i, 0)),
                  pl.BlockSpec((s, 512), lambda bi, i: (bi, 0)),
                  pl.BlockSpec((None, nch, LANES), lambda bi, i: (bi, 0, 0)),
                  pl.BlockSpec((None, nch, LANES), lambda bi, i: (bi, 0, 0)),
                  pl.BlockSpec((tq, LANES), lambda bi, i: (bi * nq + i, 0)),
                  _const(onehot_t.shape), _const(overlap_t.shape), _const(wbias.shape), _const(cbias.shape)],
        out_specs=pl.BlockSpec((tq, 512), lambda bi, i: (bi * nq + i, 0)),
        out_shape=jax.ShapeDtypeStruct((t, 512), BF16),
        scratch_shapes=[pltpu.VMEM((8 * tq, 2 * LANES), BF16), pltpu.VMEM((8 * tq, LANES), F32),
                        pltpu.VMEM((8 * tq, LANES), F32),
                        pltpu.VMEM((4 * tq, LANES), F32), pltpu.VMEM((8 * tq, TK_SEL), F32),
                        pltpu.VMEM((8 * tq, TK_SEL), F32)],
        compiler_params=_params(2),
        name="nsa",
    )(qnsa, kvsw, kc, vc, gates, onehot_t, overlap_t, wbias, cbias)

    tq = TQ_MLA
    nq = s // tq
    assert tq == TK_MLA
    dbias = jnp.asarray(np.where(np.arange(tq)[None, :] <= np.arange(tq)[:, None], 0.0, NEG)
                        .astype(np.float32))
    omla = pl.pallas_call(
        _mla_kernel,
        grid=(b, nq),
        in_specs=[pl.BlockSpec((tq, 1024), lambda bi, i: (bi * nq + i, 0)),
                  pl.BlockSpec((s, 1024), lambda bi, i: (bi, 0)),
                  pl.BlockSpec((s, 512), lambda bi, i: (bi, 0)), _const(dbias.shape)],
        out_specs=pl.BlockSpec((tq, 512), lambda bi, i: (bi * nq + i, 0)),
        out_shape=jax.ShapeDtypeStruct((t, 512), BF16),
        scratch_shapes=[pltpu.VMEM((MLA_HEADS, tq, LANES), F32), pltpu.VMEM((MLA_HEADS, tq, LANES), F32)],
        compiler_params=_params(2),
        name="mla",
    )(qmla, kmla, vmla, dbias)

    tm = TM_MERGE
    row_blk = lambda w: pl.BlockSpec((tm, w), lambda i: (i, 0))
    x1 = pl.pallas_call(
        _merge_kernel,
        grid=(t // tm,),
        in_specs=[row_blk(d), row_blk(512), row_blk(512), _const((512, d)), _const((512, d)),
                  _const((d, 2 * d)), _const((d, d)), _const((1, d)), _const((1, d))],
        out_specs=row_blk(d),
        out_shape=jax.ShapeDtypeStruct((t, d), F32),
        compiler_params=_params(1),
        name="merge",
    )(xf, onsa, omla, nsa_w_o[perm, :].astype(BF16), mla_w_o.astype(BF16), wmg, w_out.astype(BF16),
      ln1_g.reshape(1, -1), ln1_b.reshape(1, -1))

    tm = TM_FFN
    nfc = D_FF // FC_FFN
    n_s = s // tm
    conv_tab = jnp.concatenate([conv_w, conv_b.reshape(1, -1), jnp.zeros((4, D_FF), F32)], axis=0)
    blk = pl.BlockSpec((tm, d), lambda bi, i: (bi * n_s + i, 0))
    y = pl.pallas_call(
        _ffn_kernel,
        grid=(b, n_s),
        in_specs=[blk, _const((d, D_FF)), _const((d, D_FF)), _const((D_FF, d)), _const((8, D_FF)),
                  _const((1, d)), _const((1, d))],
        out_specs=blk,
        out_shape=jax.ShapeDtypeStruct((t, d), F32),
        scratch_shapes=[pltpu.VMEM((8, D_FF), F32), pltpu.VMEM((tm, D_FF), BF16)],
        compiler_params=_params(2),
        name="ffn",
    )(x1, w_gate.astype(BF16), w_up.astype(BF16), w_down.astype(BF16), conv_tab,
      ln2_g.reshape(1, -1), ln2_b.reshape(1, -1))
    return y
```

```python
import functools

import numpy as np
import jax
import jax.numpy as jnp
from jax import lax
from jax.experimental import pallas as pl
from jax.experimental.pallas import tpu as pltpu

D_MODEL = 1024
ROPE_THETA = 500000.0
NSA_HEADS = 8
NSA_KV_GROUPS = 2
NSA_GROUP = 4
NSA_HEAD_DIM = 64
NSA_ROT_DIM = 16
CMP_BLOCK = 32
CMP_STRIDE = 16
CMP_HIDDEN = 256
SEL_BLOCK = 64
SEL_TOPK = 8
WINDOW = 256
MLA_HEADS = 8
MLA_Q_RANK = 768
MLA_KV_RANK = 256
MLA_NOPE_DIM = 64
MLA_ROPE_DIM = 32
MLA_V_DIM = 64
MLA_QK_DIM = 96
D_FF = 2816
LN_EPS = 1e-5
RMS_EPS = 1e-6
DEPTH = 1
DEEPNORM_ALPHA = (2 * DEPTH) ** 0.25
LOG2E = 1.4426950408889634

LANES = 128
HALF = LANES // 2
NEG = -1e30
BIG = 3e38
VMEM_LIMIT = 56 * 1024 * 1024

TM_PROJ = 1024
TQ_NSA = 128
TK_SEL = 512
TQ_MLA = 512
TK_MLA = 512
TM_MERGE = 1024
TM_FFN = 1024
FC_FFN = 256

BF16 = jnp.bfloat16
F32 = jnp.float32


def _dot(a, b):
    return jnp.dot(a, b, preferred_element_type=F32)


def _dot_nt(a, b):
    return lax.dot_general(a, b, (((1,), (1,)), ((), ())), preferred_element_type=F32)


def _gelu_tanh(x):
    return x * (0.5 * (1.0 + jnp.tanh(0.7978845608028654 * (x + 0.044715 * (x * x * x)))))


def _sigmoid(x):
    return 1.0 / (1.0 + jnp.exp(-x))


def _layer_norm(x, g, b):
    mu = jnp.mean(x, axis=-1, keepdims=True)
    xc = x - mu
    var = jnp.mean(xc * xc, axis=-1, keepdims=True)
    return xc * lax.rsqrt(var + LN_EPS) * g + b


def _rms_norm(x, g):
    return x * lax.rsqrt(jnp.mean(x * x, axis=-1, keepdims=True) + RMS_EPS) * g


def _rope_chunk(x, cos, sin_signed, first_half, half):
    partner = jnp.where(first_half, pltpu.roll(x, LANES - half, 1), pltpu.roll(x, half, 1))
    return x * cos + partner * sin_signed


def _softmax_rows(s):
    e = jnp.exp2(s - jnp.max(s, axis=-1, keepdims=True))
    return e / jnp.sum(e, axis=-1, keepdims=True)


def _proj_kernel(x_ref, tab_ref, wq_ref, wkv_ref, wg_ref, wcq_ref, wckv_ref, wkr_ref,
                 qn_ref, kvn_ref, wuq_ref, wuk_ref, wuv_ref, place_ref,
                 qnsa_ref, kcmp_ref, vcmp_ref, kvsw_ref, gates_ref, qmla_ref, kmla_ref, vmla_ref):
    xb = x_ref[...].astype(BF16)
    tm = xb.shape[0]
    lane = lax.broadcasted_iota(jnp.int32, (tm, LANES), 1)
    nsa_first = (lane % HALF) < (NSA_ROT_DIM // 2)
    mlaq_first = lane < (MLA_NOPE_DIM + MLA_ROPE_DIM // 2)
    mlak_first = lane < (MLA_ROPE_DIM // 2)
    cn, sn = tab_ref[0], tab_ref[1]
    cq_t, sq_t = tab_ref[2], tab_ref[3]
    ck_t, sk_t = tab_ref[4], tab_ref[5]

    def rope_n(v):
        return _rope_chunk(v, cn, sn, nsa_first, NSA_ROT_DIM // 2)

    q = _dot(xb, wq_ref[...])
    kv = _dot(xb, wkv_ref[...])
    cq_raw = _dot(xb, wcq_ref[...])
    ckv_raw = _dot(xb, wckv_ref[...])
    kr_raw = _dot(xb, wkr_ref[...])
    gate_logits = _dot(xb, wg_ref[...])

    cq = _rms_norm(cq_raw, qn_ref[...]).astype(BF16)
    qm = _dot(cq, wuq_ref[...])
    ckv = _rms_norm(ckv_raw, kvn_ref[...]).astype(BF16)
    kr = _rope_chunk(kr_raw, ck_t, sk_t, mlak_first, MLA_ROPE_DIM // 2).astype(BF16)
    kmla_ref[...] = (_dot(ckv, wuk_ref[...]) + _dot(kr, place_ref[...])).astype(BF16)
    vmla_ref[...] = _dot(ckv, wuv_ref[...]).astype(BF16)

    for c in range(4):
        sl = slice(c * LANES, (c + 1) * LANES)
        qnsa_ref[:, sl] = (rope_n(q[:, sl]) * (NSA_HEAD_DIM ** -0.5 * LOG2E)).astype(BF16)

    kcmp_ref[...] = rope_n(kv[:, 0:128])
    vcmp_ref[...] = kv[:, 128:256]
    kvsw_ref[:, 0:128] = rope_n(kv[:, 256:384]).astype(BF16)
    kvsw_ref[:, 128:256] = kv[:, 384:512].astype(BF16)
    kvsw_ref[:, 256:384] = rope_n(kv[:, 512:640]).astype(BF16)
    kvsw_ref[:, 384:512] = kv[:, 640:768].astype(BF16)

    gates_ref[...] = _sigmoid(gate_logits)

    for h in range(MLA_HEADS):
        sl = slice(h * LANES, (h + 1) * LANES)
        roped = _rope_chunk(qm[:, sl], cq_t, sq_t, mlaq_first, MLA_ROPE_DIM // 2)
        qmla_ref[:, sl] = (roped * (MLA_QK_DIM ** -0.5 * LOG2E)).astype(BF16)


def _compress_kernel(kc_ref, vc_ref, pek_ref, pev_ref, wk1_ref, wv1_ref, bk_ref, bv_ref, wk2_ref, wv2_ref,
                     kout_ref, vout_ref):
    def one(tok_ref, pe_ref, w1_ref, b_ref, w2_ref, out_ref):
        nch = tok_ref.shape[0] // CMP_STRIDE
        ch = jnp.concatenate([tok_ref[pl.ds(l, nch, stride=CMP_STRIDE), :] for l in range(CMP_STRIDE)], axis=1)
        nxt = pltpu.roll(ch, nch - 1, 0)
        a = ch + pe_ref[0:1, :]
        b = nxt + pe_ref[1:2, :]
        first_group = (lax.broadcasted_iota(jnp.int32, a.shape, 1) % LANES) < HALF
        out = None
        for g in range(NSA_KV_GROUPS):
            keep = first_group if g == 0 else jnp.logical_not(first_group)
            ag = jnp.where(keep, a, 0.0).astype(BF16)
            bg = jnp.where(keep, b, 0.0).astype(BF16)
            h = _dot(ag, w1_ref[0]) + _dot(bg, w1_ref[1]) + b_ref[...]
            og = _dot(_gelu_tanh(h).astype(BF16), w2_ref[g])
            out = og if out is None else out + og
        out_ref[...] = out.astype(BF16)

    one(kc_ref, pek_ref, wk1_ref, bk_ref, wk2_ref, kout_ref)
    one(vc_ref, pev_ref, wv1_ref, bv_ref, wv2_ref, vout_ref)


class _NsaStream:
    def __init__(self, st, i, seq, refs, consts, scratch):
        q_ref, kvsw_ref, kc_ref, vc_ref, gates_ref, o_ref = refs
        self.q_ref, self.kvsw_ref, self.kc_ref, self.vc_ref = q_ref.at[st], kvsw_ref.at[st], kc_ref.at[st], vc_ref.at[st]
        self.gates_ref, self.o_ref = gates_ref.at[st], o_ref.at[st]
        self.et_ref, self.overlap_t_ref, self.wbias_ref, self.cbias_ref = consts
        qaug, m_scr, acc_scr, part_scr, sa_scr, sb_scr = scratch
        self.qaug, self.m, self.acc, self.part = qaug.at[st], m_scr.at[st], acc_scr.at[st], part_scr.at[st]
        self.sa, self.sb = sa_scr.at[st], sb_scr.at[st]
        self.i, self.seq = i, seq
        self.tq = TQ_NSA
        self.rows = 8 * TQ_NSA
        self.s0 = i * TQ_NSA
        self.lo = lax.broadcasted_iota(jnp.int32, (TQ_NSA, LANES), 1) < HALF

    def prep(self):
        tq, lo = self.tq, self.lo
        q = self.q_ref[...]
        zero = jnp.zeros((tq, LANES), BF16)
        for half in range(2):
            for c in range(4):
                r = 4 * half + c
                chunk = q[:, c * LANES:(c + 1) * LANES]
                self.qaug[r * tq:(r + 1) * tq, 0:LANES] = jnp.where(lo if half == 0 else jnp.logical_not(lo), chunk, zero)
        self.qs = self.qaug[:, 0:LANES]
        self.gates = self.gates_ref[...]

    def cmp_scores(self):
        self.sc = _dot_nt(self.qs, self.kc_ref[...])

    def cmp_softmax(self):
        tq, rows, s0 = self.tq, self.rows, self.s0
        ncmp = self.kc_ref.shape[0]
        n_idx = lax.broadcasted_iota(jnp.int32, (rows, ncmp), 1)
        pos_c = s0 + (lax.broadcasted_iota(jnp.int32, (rows, ncmp), 0) & (tq - 1))
        cmask = n_idx * CMP_STRIDE + (CMP_BLOCK - 1) <= pos_c
        sm = jnp.where(cmask, self.sc, NEG)
        e = jnp.where(cmask, jnp.exp2(sm - jnp.max(sm, axis=-1, keepdims=True)), 0.0)
        p_cmp = e / jnp.maximum(jnp.sum(e, axis=-1, keepdims=True), 1e-30)
        self.p_blocks = [p_cmp[r * tq:(r + 1) * tq] for r in range(8)]
        self.o_cmp = _dot(p_cmp.astype(BF16), self.vc_ref[...])
        self.imps = []
        for g in range(2):
            pb = self.p_blocks
            psum = (pb[4 * g] + pb[4 * g + 1]) + (pb[4 * g + 2] + pb[4 * g + 3])
            hi = psum.astype(BF16)
            lo_part = (psum - hi.astype(F32)).astype(BF16)
            self.imps.append(_dot_nt(self.overlap_t_ref[...], hi) + _dot_nt(self.overlap_t_ref[...], lo_part))

    def win_scores(self):
        self.wk = WINDOW + self.tq
        self.w0 = pl.multiple_of(jnp.maximum(self.s0 - WINDOW, 0), self.tq)
        self.sw = _dot_nt(self.qs, self.kvsw_ref[pl.ds(self.w0, self.wk), 256:384])

    def win_softmax(self):
        tq, rows = self.tq, self.rows
        wbias = self.wbias_ref[jnp.minimum(self.i, WINDOW // tq)]
        swb = jnp.concatenate([self.sw[r * tq:(r + 1) * tq] + wbias for r in range(8)], axis=0)
        e_win = jnp.exp2((swb - jnp.max(swb, axis=-1, keepdims=True)).astype(BF16))
        vw = self.kvsw_ref[pl.ds(self.w0, self.wk), 384:512]
        vw_lane = lax.broadcasted_iota(jnp.int32, vw.shape, 1)
        vw_one = jnp.ones(vw.shape, BF16)
        o_win = jnp.concatenate([_dot(e_win[:rows // 2], jnp.where(vw_lane < HALF, vw, vw_one)),
                                 _dot(e_win[rows // 2:], jnp.where(vw_lane < HALF, vw_one, vw))], axis=0)
        for c in range(4):
            self.part[c * tq:(c + 1) * tq] = (self.gate(0, c) * self.merged(self.o_cmp, c)
                                              + self.gate(2, c) * (self.merged(o_win, c) / self.merged_denominator(o_win, c)))

    def head_rows(self, a, r):
        return a[r * self.tq:(r + 1) * self.tq]

    def merged(self, a, c):
        return jnp.where(self.lo, self.head_rows(a, c), self.head_rows(a, 4 + c))

    def merged_denominator(self, a, c):
        return pltpu.roll(jnp.where(self.lo, self.head_rows(a, 4 + c), self.head_rows(a, c)), HALF, 1)

    def gate(self, branch, c):
        g = self.gates
        return jnp.where(self.lo, g[:, 8 * branch + c:8 * branch + c + 1], g[:, 8 * branch + 4 + c:8 * branch + 5 + c])

    def rank(self):
        tq, s0 = self.tq, self.s0
        n_blk = self.seq // SEL_BLOCK
        blk = lax.broadcasted_iota(jnp.int32, (n_blk, LANES), 0)
        pad = jnp.zeros((LANES - n_blk, LANES), F32)
        for g, part in [(g, part) for g in range(2) for part in range(tq // LANES)]:
            cur = (s0 + part * LANES + lax.broadcasted_iota(jnp.int32, (n_blk, LANES), 1)) // SEL_BLOCK
            valid = blk <= cur
            forced = (blk == 0) | (blk == cur) | (blk == cur - 1)
            imp = self.imps[g][:, part * LANES:(part + 1) * LANES]
            impv = jnp.where(valid, jnp.where(forced, BIG, imp), -1.0)
            n_slab = n_blk // 8
            slabs = [impv[8 * v:8 * (v + 1)] for v in range(n_slab)]
            cnts = [jnp.zeros((8, LANES), F32) for _ in range(n_slab)]
            for b in range(n_blk):
                other = impv[b:b + 1, :]
                for v in range(n_slab):
                    if b < 8 * v:
                        beats = other >= slabs[v]
                    elif b >= 8 * (v + 1):
                        beats = other > slabs[v]
                    else:
                        beats = (other > slabs[v]) | ((other == slabs[v]) & (blk[0:8] > b - 8 * v))
                    cnts[v] = cnts[v] + jnp.where(beats, 1.0, 0.0)
            cnt = jnp.concatenate(cnts, axis=0)
            bias_t = jnp.where(valid & (cnt < float(SEL_TOPK)), 0.0, NEG)
            bias = jnp.concatenate([bias_t, pad], axis=0).T.astype(BF16)
            for r in range(4 * g, 4 * g + 4):
                self.qaug[r * tq + part * LANES:r * tq + (part + 1) * LANES, LANES:2 * LANES] = bias

    def sel_init(self):
        self.m[...] = jnp.full((self.rows, LANES), NEG, F32)
        self.acc[...] = jnp.zeros((self.rows, LANES), F32)

    def sel_scores(self, kt, buf):
        k0 = pl.multiple_of(kt * TK_SEL, TK_SEL)
        k_aug = jnp.concatenate([self.kvsw_ref[pl.ds(k0, TK_SEL), 0:128], self.et_ref[pl.ds(k0, TK_SEL), :]], axis=1)
        buf[...] = _dot_nt(self.qaug[...], k_aug)

    def sel_tile(self, kt, buf, diagonal):
        tq, rows = self.tq, self.rows
        k0 = pl.multiple_of(kt * TK_SEL, TK_SEL)
        v = self.kvsw_ref[pl.ds(k0, TK_SEL), 128:256]
        s = buf[...]
        if diagonal:
            cb = self.cbias_ref[(self.s0 - k0) // tq]
            s = jnp.concatenate([s[r * tq:(r + 1) * tq] + cb for r in range(8)], axis=0)
        m_prev = self.m[...]
        m_new = jnp.maximum(m_prev, jnp.max(s, axis=-1, keepdims=True))
        alpha = jnp.exp2(m_prev - m_new)
        e = jnp.exp2((s - jnp.tile(m_new, (1, TK_SEL // LANES))).astype(BF16))
        v_lane = lax.broadcasted_iota(jnp.int32, v.shape, 1)
        one = jnp.ones(v.shape, BF16)
        pv = jnp.concatenate([_dot(e[:rows // 2], jnp.where(v_lane < HALF, v, one)),
                              _dot(e[rows // 2:], jnp.where(v_lane < HALF, one, v))], axis=0)
        self.acc[...] = alpha * self.acc[...] + pv
        self.m[...] = m_new

    def finish(self):
        tq = self.tq
        acc = self.acc[...]
        for c in range(4):
            o_sel = self.merged(acc, c) / self.merged_denominator(acc, c)
            self.o_ref[:, c * LANES:(c + 1) * LANES] = (self.part[c * tq:(c + 1) * tq] + self.gate(1, c) * o_sel).astype(BF16)


def _nsa_kernel(q_ref, kvsw_ref, kc_ref, vc_ref, gates_ref, et_ref, overlap_t_ref, wbias_ref, cbias_ref, o_ref,
                qaug_scr, m_scr, acc_scr, part_scr, sa_scr, sb_scr, *, seq):
    i = pl.program_id(1)
    a, b = [_NsaStream(st, i, seq, (q_ref, kvsw_ref, kc_ref, vc_ref, gates_ref, o_ref),
                       (et_ref, overlap_t_ref, wbias_ref, cbias_ref),
                       (qaug_scr, m_scr, acc_scr, part_scr, sa_scr, sb_scr)) for st in range(2)]
    a.prep()
    b.prep()
    a.cmp_scores()
    b.cmp_scores()
    a.win_scores()
    a.cmp_softmax()
    b.win_scores()
    b.cmp_softmax()
    a.rank()
    a.win_softmax()
    a.sel_init()
    a.sel_scores(0, a.sa)
    b.rank()
    b.win_softmax()
    b.sel_init()
    b.sel_scores(0, b.sa)

    last = (i * TQ_NSA) // TK_SEL

    def tile_pair(p, carry):
        kt = 2 * p
        for s in (a, b):
            s.sel_scores(kt + 1, s.sb)
        for s in (a, b):
            s.sel_tile(kt, s.sa, False)
        for s in (a, b):
            s.sel_scores(kt + 2, s.sa)
        for s in (a, b):
            s.sel_tile(kt + 1, s.sb, False)
        return carry

    lax.fori_loop(0, last // 2, tile_pair, 0)
    last_is_odd = (last & 1) == 1

    @pl.when(last_is_odd)
    def _():
        for s in (a, b):
            s.sel_scores(last, s.sb)
        for s in (a, b):
            s.sel_tile(last - 1, s.sa, False)
        for s in (a, b):
            s.sel_tile(last, s.sb, True)

    @pl.when(jnp.logical_not(last_is_odd))
    def _():
        for s in (a, b):
            s.sel_tile(last, s.sa, True)

    a.finish()
    b.finish()


def _mla_kernel(q_ref, k_ref, v_ref, dbias_ref, o_ref, m_scr, acc_scr):
    tq = TQ_MLA
    i = pl.program_id(1)
    s0 = i * tq
    m_scr[...] = jnp.full(m_scr.shape, NEG, F32)
    acc_scr[...] = jnp.zeros(acc_scr.shape, F32)

    def head_lanes(h):
        return slice(h * LANES, (h + 1) * LANES)

    def values(h, k0, nk):
        v2 = v_ref[pl.ds(k0, nk), head_lanes(h // 2)]
        lane = lax.broadcasted_iota(jnp.int32, (nk, LANES), 1)
        one = jnp.ones((nk, LANES), BF16)
        return jnp.where(lane < HALF, v2, one) if h % 2 == 0 else jnp.where(lane < HALF, one, v2)

    def update(h, r0, nr, s, vh):
        m_prev = m_scr[h, r0:r0 + nr]
        m_new = jnp.maximum(m_prev, jnp.max(s, axis=-1, keepdims=True))
        alpha = jnp.exp2(m_prev - m_new)
        e = jnp.exp2((s - jnp.tile(m_new, (1, s.shape[1] // LANES))).astype(BF16))
        acc_scr[h, r0:r0 + nr] = alpha * acc_scr[h, r0:r0 + nr] + _dot(e, vh)
        m_scr[h, r0:r0 + nr] = m_new

    def full_tile(kt, carry):
        k0 = pl.multiple_of(kt * TK_MLA, TK_MLA)
        scores = [_dot_nt(q_ref[:, head_lanes(h)], k_ref[pl.ds(k0, TK_MLA), head_lanes(h)])
                  for h in range(MLA_HEADS)]
        for h in range(MLA_HEADS):
            update(h, 0, tq, scores[h], values(h, k0, TK_MLA))
        return carry

    lax.fori_loop(0, s0 // TK_MLA, full_tile, 0)

    kd = pl.multiple_of(s0, tq)
    dbias = dbias_ref[...]
    scores = [_dot_nt(q_ref[:, head_lanes(h)], k_ref[pl.ds(kd, tq), head_lanes(h)]) for h in range(MLA_HEADS)]
    for h in range(MLA_HEADS):
        update(h, 0, tq, scores[h] + dbias, values(h, kd, tq))
    lo = lax.broadcasted_iota(jnp.int32, (tq, LANES), 1) < HALF
    for c in range(4):
        mixed = jnp.where(lo, acc_scr[2 * c], acc_scr[2 * c + 1])
        denom = jnp.where(lo, acc_scr[2 * c + 1], acc_scr[2 * c])
        o_ref[:, c * LANES:(c + 1) * LANES] = (mixed / pltpu.roll(denom, HALF, 1)).astype(BF16)


def _merge_kernel(x_ref, onsa_ref, omla_ref, wa_ref, wb_ref, wmg_ref, wout_ref, g_ref, b_ref, y_ref):
    n_sub = 4
    sub = x_ref.shape[0] // n_sub
    first = []
    for j in range(n_sub):
        rs = slice(j * sub, (j + 1) * sub)
        first.append((_dot(onsa_ref[rs, :], wa_ref[...]), _dot(omla_ref[rs, :], wb_ref[...]),
                      _dot(x_ref[rs, :].astype(BF16), wmg_ref[...])))
    for j in range(n_sub):
        rs = slice(j * sub, (j + 1) * sub)
        ya, yb, mg_logits = first[j]
        mg = _sigmoid(mg_logits)
        mixed = (mg[:, :D_MODEL] * ya + mg[:, D_MODEL:] * yb).astype(BF16)
        y_ref[rs, :] = _layer_norm(DEEPNORM_ALPHA * x_ref[rs, :] + _dot(mixed, wout_ref[...]), g_ref[...], b_ref[...])


def _ffn_kernel(x_ref, wg_ref, wu_ref, wd_ref, cw_ref, g_ref, b_ref, y_ref, tail_scr, h_scr):
    tm = TM_FFN
    nfc = D_FF // FC_FFN
    first_tile = pl.program_id(1) == 0
    x = x_ref[...]
    xb = x.astype(BF16)
    row = lax.broadcasted_iota(jnp.int32, (tm, FC_FFN), 0)

    @pl.when(first_tile)
    def _():
        tail_scr[...] = jnp.zeros(tail_scr.shape, F32)

    def gate_up(fc):
        cols = slice(fc * FC_FFN, (fc + 1) * FC_FFN)
        return _dot(xb, wg_ref[:, cols]), _dot(xb, wu_ref[:, cols])

    cur = gate_up(0)
    for fc in range(nfc):
        cols = slice(fc * FC_FFN, (fc + 1) * FC_FFN)
        nxt = gate_up(fc + 1) if fc + 1 < nfc else None
        a, up = cur
        tail = tail_scr[:, cols]
        tail_scr[:, cols] = a[tm - 8:, :]
        a1 = jnp.where(row == 0, tail[7:8, :], pltpu.roll(a, 1, 0))
        a2 = jnp.where(row == 0, tail[6:7, :], jnp.where(row == 1, tail[7:8, :], pltpu.roll(a, 2, 0)))
        cw = cw_ref[:, cols]
        conv = cw[0:1, :] * a2 + cw[1:2, :] * a1 + cw[2:3, :] * a + cw[3:4, :]
        h_scr[:, cols] = (_gelu_tanh(conv) * up).astype(BF16)
        cur = nxt
    half = tm // 2
    ys = [_dot(h_scr[j * half:(j + 1) * half, :], wd_ref[...]) for j in range(2)]
    for j in range(2):
        rs = slice(j * half, (j + 1) * half)
        y_ref[rs, :] = _layer_norm(DEEPNORM_ALPHA * x_ref[rs, :] + ys[j], g_ref[...], b_ref[...])


def _rope_tables(seq):
    pos = np.arange(seq, dtype=np.float64)

    def cs(rot):
        inv = ROPE_THETA ** (-np.arange(0, rot, 2, dtype=np.float64) / rot)
        ang = pos[:, None] * inv[None, :]
        return np.cos(ang), np.sin(ang)

    one = lambda n: np.ones((seq, n))
    zero = lambda n: np.zeros((seq, n))
    c8, s8 = cs(NSA_ROT_DIM)
    c16, s16 = cs(MLA_ROPE_DIM)
    nsa_c = np.concatenate([c8, c8, one(48)] * 2, axis=1)
    nsa_s = np.concatenate([-s8, s8, zero(48)] * 2, axis=1)
    mq_c = np.concatenate([one(64), c16, c16, one(32)], axis=1)
    mq_s = np.concatenate([zero(64), -s16, s16, zero(32)], axis=1)
    mk_c = np.concatenate([c16, c16, one(96)], axis=1)
    mk_s = np.concatenate([-s16, s16, zero(96)], axis=1)
    return jnp.asarray(np.stack([nsa_c, nsa_s, mq_c, mq_s, mk_c, mk_s]).astype(np.float32))


def _head_perm():
    return np.array([(c + 4 * half) * 64 + d for c in range(4) for half in range(2) for d in range(64)])


def _window_bias(tq):
    wk = WINDOW + tq
    out = np.zeros((WINDOW // tq + 1, tq, wk), np.float32)
    for v in range(WINDOW // tq + 1):
        s0 = v * tq
        w0 = max(s0 - WINDOW, 0)
        pos = s0 + np.arange(tq)[:, None]
        kpos = w0 + np.arange(wk)[None, :]
        out[v] = np.where((kpos <= pos) & (pos - kpos < WINDOW), 0.0, NEG)
    return jnp.asarray(out)


def _const(shape):
    nd = len(shape)
    return pl.BlockSpec(shape, lambda *_: (0,) * nd, pipeline_mode=pl.Buffered(1))


def _params(n_axes):
    return pltpu.CompilerParams(dimension_semantics=("arbitrary",) * n_axes, vmem_limit_bytes=VMEM_LIMIT)


def kernel(x, w_in, cmp_pe_k, cmp_pe_v, cmp_k_w1, cmp_k_b1, cmp_k_w2, cmp_v_w1, cmp_v_b1, cmp_v_w2,
           nsa_w_o, mla_q_norm, mla_w_uq, mla_kv_norm, mla_w_uk, mla_w_uv, mla_w_o, w_out,
           ln1_g, ln1_b, ffn_w_gate, ffn_w_up, ffn_conv_w, ffn_conv_b, ffn_w_down, ln2_g, ln2_b):
    b, s, d = x.shape
    assert d == D_MODEL and s % TK_SEL == 0 and s % TM_FFN == 0 and (s // CMP_STRIDE) % 8 == 0
    assert WINDOW % TQ_NSA == 0 and s >= WINDOW + TQ_NSA
    t = b * s
    perm = _head_perm()
    xf = x.reshape(t, d)
    for l in range(DEPTH):
        xf = _layer(xf, b, s, perm, w_in[l], cmp_pe_k[l], cmp_pe_v[l], cmp_k_w1[l], cmp_k_b1[l], cmp_k_w2[l],
                    cmp_v_w1[l], cmp_v_b1[l], cmp_v_w2[l], nsa_w_o[l], mla_q_norm[l], mla_w_uq[l],
                    mla_kv_norm[l], mla_w_uk[l], mla_w_uv[l], mla_w_o[l], w_out[l], ln1_g[l], ln1_b[l],
                    ffn_w_gate[l], ffn_w_up[l], ffn_conv_w[l], ffn_conv_b[l], ffn_w_down[l], ln2_g[l], ln2_b[l])
    return xf.reshape(b, s, d)


def _layer(xf, b, s, perm, w_in, pe_k, pe_v, ck_w1, ck_b1, ck_w2, cv_w1, cv_b1, cv_w2, nsa_w_o, q_norm, w_uq,
           kv_norm, w_uk, w_uv, mla_w_o, w_out, ln1_g, ln1_b, w_gate, w_up, conv_w, conv_b, w_down, ln2_g, ln2_b):
    t = b * s
    d = D_MODEL
    o = np.cumsum([0, 512, 128, 128, 128, 128, 128, 128, 24, MLA_Q_RANK, MLA_KV_RANK, MLA_ROPE_DIM, 2 * d])
    wq = w_in[:, o[0]:o[1]][:, perm].astype(BF16)
    wkv = w_in[:, o[1]:o[7]].astype(BF16)
    gate_cols = np.array([h * 3 + br for br in range(3) for h in range(8)])
    wg = jnp.pad(w_in[:, o[7]:o[8]][:, gate_cols], ((0, 0), (0, LANES - 24))).astype(BF16)
    wcq = w_in[:, o[8]:o[9]].astype(BF16)
    wckv = w_in[:, o[9]:o[10]].astype(BF16)
    wkr = jnp.pad(w_in[:, o[10]:o[11]], ((0, 0), (0, LANES - MLA_ROPE_DIM))).astype(BF16)
    wmg = w_in[:, o[11]:o[12]].astype(BF16)

    wuq = jnp.pad(w_uq.reshape(MLA_Q_RANK, MLA_HEADS, MLA_QK_DIM), ((0, 0), (0, 0), (0, LANES - MLA_QK_DIM)))
    wuq = wuq.reshape(MLA_Q_RANK, MLA_HEADS * LANES).astype(BF16)
    wuk = jnp.pad(w_uk.reshape(MLA_KV_RANK, MLA_HEADS, MLA_NOPE_DIM), ((0, 0), (0, 0), (0, LANES - MLA_NOPE_DIM)))
    wuk = wuk.reshape(MLA_KV_RANK, MLA_HEADS * LANES).astype(BF16)
    place = np.zeros((LANES, MLA_HEADS * LANES), np.float32)
    for h in range(MLA_HEADS):
        for r in range(MLA_ROPE_DIM):
            place[r, h * LANES + MLA_NOPE_DIM + r] = 1.0
    place = jnp.asarray(place, BF16)
    tabs = _rope_tables(s)

    tm = TM_PROJ
    n_s = s // tm
    row_blk = lambda w: pl.BlockSpec((tm, w), lambda i: (i, 0))
    outs = pl.pallas_call(
        _proj_kernel,
        grid=(t // tm,),
        in_specs=[row_blk(d), pl.BlockSpec((6, tm, LANES), lambda i: (0, i % n_s, 0)),
                  _const(wq.shape), _const(wkv.shape), _const(wg.shape), _const(wcq.shape), _const(wckv.shape),
                  _const(wkr.shape), _const((1, MLA_Q_RANK)), _const((1, MLA_KV_RANK)), _const(wuq.shape),
                  _const(wuk.shape), _const((MLA_KV_RANK, 512)), _const(place.shape)],
        out_specs=[row_blk(512), row_blk(128), row_blk(128), row_blk(512), row_blk(128),
                   row_blk(1024), row_blk(1024), row_blk(512)],
        out_shape=[jax.ShapeDtypeStruct((t, 512), BF16), jax.ShapeDtypeStruct((t, 128), F32),
                   jax.ShapeDtypeStruct((t, 128), F32), jax.ShapeDtypeStruct((t, 512), BF16),
                   jax.ShapeDtypeStruct((t, 128), F32), jax.ShapeDtypeStruct((t, 1024), BF16),
                   jax.ShapeDtypeStruct((t, 1024), BF16), jax.ShapeDtypeStruct((t, 512), BF16)],
        compiler_params=_params(1),
        name="proj",
    )(xf, tabs, wq, wkv, wg, wcq, wckv, wkr, q_norm.reshape(1, -1), kv_norm.reshape(1, -1), wuq, wuk,
      w_uv.astype(BF16), place)
    qnsa, kcmp, vcmp, kvsw, gates, qmla, kmla, vmla = outs

    nch = s // CMP_STRIDE
    flat = CMP_STRIDE * LANES

    def dup_w1(w1):
        w = w1.astype(BF16).reshape(2, CMP_STRIDE, 1, NSA_HEAD_DIM, CMP_HIDDEN)
        return jnp.broadcast_to(w, (2, CMP_STRIDE, 2, NSA_HEAD_DIM, CMP_HIDDEN)).reshape(2, flat, CMP_HIDDEN)

    def big_pe(pe):
        return jnp.tile(pe.reshape(2, CMP_STRIDE, 1, NSA_HEAD_DIM), (1, 1, 2, 1)).reshape(2, flat)

    def placed_w2(w2):
        w = w2.astype(BF16)
        z = jnp.zeros_like(w)
        return jnp.stack([jnp.concatenate([w, z], axis=1), jnp.concatenate([z, w], axis=1)])

    tok_blk = pl.BlockSpec((s, LANES), lambda i: (i, 0))
    cmp_blk = pl.BlockSpec((None, nch, LANES), lambda i: (i, 0, 0))
    kc, vc = pl.pallas_call(
        _compress_kernel,
        grid=(b,),
        in_specs=[tok_blk, tok_blk, _const((2, flat)), _const((2, flat)),
                  _const((2, flat, CMP_HIDDEN)), _const((2, flat, CMP_HIDDEN)),
                  _const((1, CMP_HIDDEN)), _const((1, CMP_HIDDEN)),
                  _const((2, CMP_HIDDEN, LANES)), _const((2, CMP_HIDDEN, LANES))],
        out_specs=[cmp_blk, cmp_blk],
        out_shape=[jax.ShapeDtypeStruct((b, nch, LANES), BF16)] * 2,
        compiler_params=_params(1),
        name="compress",
    )(kcmp, vcmp, big_pe(pe_k), big_pe(pe_v), dup_w1(ck_w1), dup_w1(cv_w1),
      ck_b1.reshape(1, -1), cv_b1.reshape(1, -1), placed_w2(ck_w2), placed_w2(cv_w2))

    n_sel = s // SEL_BLOCK
    key_blk = np.arange(s) // SEL_BLOCK
    onehot_t = jnp.asarray((key_blk[:, None] == np.arange(LANES)[None, :]).astype(np.float32), BF16)
    n_i = np.arange(nch)[None, :]
    j_i = np.arange(n_sel)[:, None]
    overlap_t = ((n_i <= 4 * j_i + 3) & (n_i >= 4 * j_i - 1) & (n_i < nch - 1)).astype(np.float32)
    overlap_t = jnp.asarray(overlap_t, BF16)
    tq = TQ_NSA
    nq = s // tq
    wbias = _window_bias(tq)
    r_i = np.arange(tq)[None, :, None]
    c_i = np.arange(TK_SEL)[None, None, :]
    v_i = np.arange(TK_SEL // tq)[:, None, None]
    cbias = jnp.asarray(np.where(c_i <= v_i * tq + r_i, 0.0, NEG).astype(np.float32))
    assert b % 2 == 0
    pair = lambda arr: arr.reshape((b // 2, 2) + arr.shape[1:])
    seq_blk = lambda rows_, w: pl.BlockSpec((None, 2, rows_, w), lambda bi, i: (bi, 0, i, 0))
    whole_blk = lambda rows_, w: pl.BlockSpec((None, 2, rows_, w), lambda bi, i: (bi, 0, 0, 0))
    onsa = pl.pallas_call(
        functools.partial(_nsa_kernel, seq=s),
        grid=(b // 2, nq),
        in_specs=[seq_blk(tq, 512), whole_blk(s, 512), whole_blk(nch, LANES), whole_blk(nch, LANES),
                  seq_blk(tq, LANES),
                  _const(onehot_t.shape), _const(overlap_t.shape), _const(wbias.shape), _const(cbias.shape)],
        out_specs=seq_blk(tq, 512),
        out_shape=jax.ShapeDtypeStruct((b // 2, 2, s, 512), BF16),
        scratch_shapes=[pltpu.VMEM((2, 8 * tq, 2 * LANES), BF16), pltpu.VMEM((2, 8 * tq, LANES), F32),
                        pltpu.VMEM((2, 8 * tq, LANES), F32),
                        pltpu.VMEM((2, 4 * tq, LANES), F32), pltpu.VMEM((2, 8 * tq, TK_SEL), F32),
                        pltpu.VMEM((2, 8 * tq, TK_SEL), F32)],
        compiler_params=_params(2),
        name="nsa",
    )(pair(qnsa.reshape(b, s, 512)), pair(kvsw.reshape(b, s, 512)), pair(kc), pair(vc),
      pair(gates.reshape(b, s, LANES)), onehot_t, overlap_t, wbias, cbias).reshape(t, 512)

    tq = TQ_MLA
    nq = s // tq
    assert tq == TK_MLA
    dbias = jnp.asarray(np.where(np.arange(tq)[None, :] <= np.arange(tq)[:, None], 0.0, NEG)
                        .astype(np.float32))
    omla = pl.pallas_call(
        _mla_kernel,
        grid=(b, nq),
        in_specs=[pl.BlockSpec((tq, 1024), lambda bi, i: (bi * nq + i, 0)),
                  pl.BlockSpec((s, 1024), lambda bi, i: (bi, 0)),
                  pl.BlockSpec((s, 512), lambda bi, i: (bi, 0)), _const(dbias.shape)],
        out_specs=pl.BlockSpec((tq, 512), lambda bi, i: (bi * nq + i, 0)),
        out_shape=jax.ShapeDtypeStruct((t, 512), BF16),
        scratch_shapes=[pltpu.VMEM((MLA_HEADS, tq, LANES), F32), pltpu.VMEM((MLA_HEADS, tq, LANES), F32)],
        compiler_params=_params(2),
        name="mla",
    )(qmla, kmla, vmla, dbias)

    tm = TM_MERGE
    row_blk = lambda w: pl.BlockSpec((tm, w), lambda i: (i, 0))
    x1 = pl.pallas_call(
        _merge_kernel,
        grid=(t // tm,),
        in_specs=[row_blk(d), row_blk(512), row_blk(512), _const((512, d)), _const((512, d)),
                  _const((d, 2 * d)), _const((d, d)), _const((1, d)), _const((1, d))],
        out_specs=row_blk(d),
        out_shape=jax.ShapeDtypeStruct((t, d), F32),
        compiler_params=_params(1),
        name="merge",
    )(xf, onsa, omla, nsa_w_o[perm, :].astype(BF16), mla_w_o.astype(BF16), wmg, w_out.astype(BF16),
      ln1_g.reshape(1, -1), ln1_b.reshape(1, -1))

    tm = TM_FFN
    nfc = D_FF // FC_FFN
    n_s = s // tm
    conv_tab = jnp.concatenate([conv_w, conv_b.reshape(1, -1), jnp.zeros((4, D_FF), F32)], axis=0)
    blk = pl.BlockSpec((tm, d), lambda bi, i: (bi * n_s + i, 0))
    y = pl.pallas_call(
        _ffn_kernel,
        grid=(b, n_s),
        in_specs=[blk, _const((d, D_FF)), _const((d, D_FF)), _const((D_FF, d)), _const((8, D_FF)),
                  _const((1, d)), _const((1, d))],
        out_specs=blk,
        out_shape=jax.ShapeDtypeStruct((t, d), F32),
        scratch_shapes=[pltpu.VMEM((8, D_FF), F32), pltpu.VMEM((tm, D_FF), BF16)],
        compiler_params=_params(2),
        name="ffn",
    )(x1, w_gate.astype(BF16), w_up.astype(BF16), w_down.astype(BF16), conv_tab,
      ln2_g.reshape(1, -1), ln2_b.reshape(1, -1))
    return y
```

```python
import functools

import numpy as np
import jax
import jax.numpy as jnp
from jax import lax
from jax.experimental import pallas as pl
from jax.experimental.pallas import tpu as pltpu

D_MODEL = 1024
ROPE_THETA = 500000.0
NSA_HEADS = 8
NSA_KV_GROUPS = 2
NSA_GROUP = 4
NSA_HEAD_DIM = 64
NSA_ROT_DIM = 16
CMP_BLOCK = 32
CMP_STRIDE = 16
CMP_HIDDEN = 256
SEL_BLOCK = 64
SEL_TOPK = 8
WINDOW = 256
MLA_HEADS = 8
MLA_Q_RANK = 768
MLA_KV_RANK = 256
MLA_NOPE_DIM = 64
MLA_ROPE_DIM = 32
MLA_V_DIM = 64
MLA_QK_DIM = 96
D_FF = 2816
LN_EPS = 1e-5
RMS_EPS = 1e-6
DEPTH = 1
DEEPNORM_ALPHA = (2 * DEPTH) ** 0.25
LOG2E = 1.4426950408889634

LANES = 128
HALF = LANES // 2
NEG = -1e30
BIG = 3e38
VMEM_LIMIT = 56 * 1024 * 1024

TM_PROJ = 1024
TQ_NSA = 128
TK_SEL = 512
TQ_MLA = 512
TK_MLA = 512
TM_MERGE = 1024
TM_FFN = 1024
FC_FFN = 256

BF16 = jnp.bfloat16
F32 = jnp.float32


def _dot(a, b):
    return jnp.dot(a, b, preferred_element_type=F32)


def _dot_nt(a, b):
    return lax.dot_general(a, b, (((1,), (1,)), ((), ())), preferred_element_type=F32)


def _gelu_tanh(x):
    return x * (0.5 * (1.0 + jnp.tanh(0.7978845608028654 * (x + 0.044715 * (x * x * x)))))


def _sigmoid(x):
    return 1.0 / (1.0 + jnp.exp(-x))


def _layer_norm(x, g, b):
    mu = jnp.mean(x, axis=-1, keepdims=True)
    xc = x - mu
    var = jnp.mean(xc * xc, axis=-1, keepdims=True)
    return xc * lax.rsqrt(var + LN_EPS) * g + b


def _rms_norm(x, g):
    return x * lax.rsqrt(jnp.mean(x * x, axis=-1, keepdims=True) + RMS_EPS) * g


def _rope_chunk(x, cos, sin_signed, first_half, half):
    partner = jnp.where(first_half, pltpu.roll(x, LANES - half, 1), pltpu.roll(x, half, 1))
    return x * cos + partner * sin_signed


def _softmax_rows(s):
    e = jnp.exp2(s - jnp.max(s, axis=-1, keepdims=True))
    return e / jnp.sum(e, axis=-1, keepdims=True)


def _proj_kernel(x_ref, tab_ref, wq_ref, wkv_ref, wg_ref, wcq_ref, wckv_ref, wkr_ref,
                 qn_ref, kvn_ref, wuq_ref, wuk_ref, wuv_ref, place_ref,
                 qnsa_ref, kcmp_ref, vcmp_ref, kvsw_ref, gates_ref, qmla_ref, kmla_ref, vmla_ref):
    xb = x_ref[...].astype(BF16)
    tm = xb.shape[0]
    lane = lax.broadcasted_iota(jnp.int32, (tm, LANES), 1)
    nsa_first = (lane % HALF) < (NSA_ROT_DIM // 2)
    mlaq_first = lane < (MLA_NOPE_DIM + MLA_ROPE_DIM // 2)
    mlak_first = lane < (MLA_ROPE_DIM // 2)
    cn, sn = tab_ref[0], tab_ref[1]
    cq_t, sq_t = tab_ref[2], tab_ref[3]
    ck_t, sk_t = tab_ref[4], tab_ref[5]

    def rope_n(v):
        return _rope_chunk(v, cn, sn, nsa_first, NSA_ROT_DIM // 2)

    q = _dot(xb, wq_ref[...])
    kv = _dot(xb, wkv_ref[...])
    cq_raw = _dot(xb, wcq_ref[...])
    ckv_raw = _dot(xb, wckv_ref[...])
    kr_raw = _dot(xb, wkr_ref[...])
    gate_logits = _dot(xb, wg_ref[...])

    cq = _rms_norm(cq_raw, qn_ref[...]).astype(BF16)
    qm = _dot(cq, wuq_ref[...])
    ckv = _rms_norm(ckv_raw, kvn_ref[...]).astype(BF16)
    kr = _rope_chunk(kr_raw, ck_t, sk_t, mlak_first, MLA_ROPE_DIM // 2).astype(BF16)
    kmla_ref[...] = (_dot(ckv, wuk_ref[...]) + _dot(kr, place_ref[...])).astype(BF16)
    vmla_ref[...] = _dot(ckv, wuv_ref[...]).astype(BF16)

    for c in range(4):
        sl = slice(c * LANES, (c + 1) * LANES)
        qnsa_ref[:, sl] = (rope_n(q[:, sl]) * (NSA_HEAD_DIM ** -0.5 * LOG2E)).astype(BF16)

    kcmp_ref[...] = rope_n(kv[:, 0:128])
    vcmp_ref[...] = kv[:, 128:256]
    kvsw_ref[:, 0:128] = rope_n(kv[:, 256:384]).astype(BF16)
    kvsw_ref[:, 128:256] = kv[:, 384:512].astype(BF16)
    kvsw_ref[:, 256:384] = rope_n(kv[:, 512:640]).astype(BF16)
    kvsw_ref[:, 384:512] = kv[:, 640:768].astype(BF16)

    gates_ref[...] = _sigmoid(gate_logits)

    for h in range(MLA_HEADS):
        sl = slice(h * LANES, (h + 1) * LANES)
        roped = _rope_chunk(qm[:, sl], cq_t, sq_t, mlaq_first, MLA_ROPE_DIM // 2)
        qmla_ref[:, sl] = (roped * (MLA_QK_DIM ** -0.5 * LOG2E)).astype(BF16)


def _compress_kernel(kc_ref, vc_ref, pek_ref, pev_ref, wk1_ref, wv1_ref, bk_ref, bv_ref, wk2_ref, wv2_ref,
                     kout_ref, vout_ref):
    def one(tok_ref, pe_ref, w1_ref, b_ref, w2_ref, out_ref):
        nch = tok_ref.shape[0] // CMP_STRIDE
        ch = jnp.concatenate([tok_ref[pl.ds(l, nch, stride=CMP_STRIDE), :] for l in range(CMP_STRIDE)], axis=1)
        nxt = pltpu.roll(ch, nch - 1, 0)
        a = ch + pe_ref[0:1, :]
        b = nxt + pe_ref[1:2, :]
        first_group = (lax.broadcasted_iota(jnp.int32, a.shape, 1) % LANES) < HALF
        out = None
        for g in range(NSA_KV_GROUPS):
            keep = first_group if g == 0 else jnp.logical_not(first_group)
            ag = jnp.where(keep, a, 0.0).astype(BF16)
            bg = jnp.where(keep, b, 0.0).astype(BF16)
            h = _dot(ag, w1_ref[0]) + _dot(bg, w1_ref[1]) + b_ref[...]
            og = _dot(_gelu_tanh(h).astype(BF16), w2_ref[g])
            out = og if out is None else out + og
        out_ref[...] = out.astype(BF16)

    one(kc_ref, pek_ref, wk1_ref, bk_ref, wk2_ref, kout_ref)
    one(vc_ref, pev_ref, wv1_ref, bv_ref, wv2_ref, vout_ref)


class _NsaStream:
    def __init__(self, st, i, seq, refs, consts, scratch):
        q_ref, kvsw_ref, kc_ref, vc_ref, gates_ref, o_ref = refs
        self.q_ref, self.kvsw_ref, self.kc_ref, self.vc_ref = q_ref.at[st], kvsw_ref.at[st], kc_ref.at[st], vc_ref.at[st]
        self.gates_ref, self.o_ref = gates_ref.at[st], o_ref.at[st]
        self.et_ref, self.overlap_t_ref, self.wbias_ref, self.cbias_ref = consts
        qaug, m_scr, acc_scr, part_scr, sa_scr, sb_scr = scratch
        self.qaug, self.m, self.acc, self.part = qaug.at[st], m_scr.at[st], acc_scr.at[st], part_scr.at[st]
        self.sa, self.sb = sa_scr.at[st], sb_scr.at[st]
        self.i, self.seq = i, seq
        self.tq = TQ_NSA
        self.rows = 8 * TQ_NSA
        self.s0 = i * TQ_NSA
        self.lo = lax.broadcasted_iota(jnp.int32, (TQ_NSA, LANES), 1) < HALF

    def prep(self):
        tq, lo = self.tq, self.lo
        q = self.q_ref[...]
        zero = jnp.zeros((tq, LANES), BF16)
        for half in range(2):
            for c in range(4):
                r = 4 * half + c
                chunk = q[:, c * LANES:(c + 1) * LANES]
                self.qaug[r * tq:(r + 1) * tq, 0:LANES] = jnp.where(lo if half == 0 else jnp.logical_not(lo), chunk, zero)
        self.qs = self.qaug[:, 0:LANES]
        self.gates = self.gates_ref[...]

    def cmp_scores(self):
        self.sc = _dot_nt(self.qs, self.kc_ref[...])

    def cmp_softmax(self):
        tq, rows, s0 = self.tq, self.rows, self.s0
        ncmp = self.kc_ref.shape[0]
        n_idx = lax.broadcasted_iota(jnp.int32, (rows, ncmp), 1)
        pos_c = s0 + (lax.broadcasted_iota(jnp.int32, (rows, ncmp), 0) & (tq - 1))
        cmask = n_idx * CMP_STRIDE + (CMP_BLOCK - 1) <= pos_c
        sm = jnp.where(cmask, self.sc, NEG)
        e = jnp.where(cmask, jnp.exp2(sm - jnp.max(sm, axis=-1, keepdims=True)), 0.0)
        p_cmp = e / jnp.maximum(jnp.sum(e, axis=-1, keepdims=True), 1e-30)
        self.p_blocks = [p_cmp[r * tq:(r + 1) * tq] for r in range(8)]
        self.o_cmp = _dot(p_cmp.astype(BF16), self.vc_ref[...])
        self.imps = []
        for g in range(2):
            pb = self.p_blocks
            psum = (pb[4 * g] + pb[4 * g + 1]) + (pb[4 * g + 2] + pb[4 * g + 3])
            hi = psum.astype(BF16)
            lo_part = (psum - hi.astype(F32)).astype(BF16)
            self.imps.append(_dot_nt(self.overlap_t_ref[...], hi) + _dot_nt(self.overlap_t_ref[...], lo_part))

    def win_scores(self):
        self.wk = WINDOW + self.tq
        self.w0 = pl.multiple_of(jnp.maximum(self.s0 - WINDOW, 0), self.tq)
        self.sw = _dot_nt(self.qs, self.kvsw_ref[pl.ds(self.w0, self.wk), 256:384])

    def win_softmax(self):
        tq, rows = self.tq, self.rows
        wbias = self.wbias_ref[jnp.minimum(self.i, WINDOW // tq)]
        swb = jnp.concatenate([self.sw[r * tq:(r + 1) * tq] + wbias for r in range(8)], axis=0)
        e_win = jnp.exp2((swb - jnp.max(swb, axis=-1, keepdims=True)).astype(BF16))
        vw = self.kvsw_ref[pl.ds(self.w0, self.wk), 384:512]
        vw_lane = lax.broadcasted_iota(jnp.int32, vw.shape, 1)
        vw_one = jnp.ones(vw.shape, BF16)
        o_win = jnp.concatenate([_dot(e_win[:rows // 2], jnp.where(vw_lane < HALF, vw, vw_one)),
                                 _dot(e_win[rows // 2:], jnp.where(vw_lane < HALF, vw_one, vw))], axis=0)
        for c in range(4):
            self.part[c * tq:(c + 1) * tq] = (self.gate(0, c) * self.merged(self.o_cmp, c)
                                              + self.gate(2, c) * (self.merged(o_win, c) / self.merged_denominator(o_win, c)))

    def head_rows(self, a, r):
        return a[r * self.tq:(r + 1) * self.tq]

    def merged(self, a, c):
        return jnp.where(self.lo, self.head_rows(a, c), self.head_rows(a, 4 + c))

    def merged_denominator(self, a, c):
        return pltpu.roll(jnp.where(self.lo, self.head_rows(a, 4 + c), self.head_rows(a, c)), HALF, 1)

    def gate(self, branch, c):
        g = self.gates
        return jnp.where(self.lo, g[:, 8 * branch + c:8 * branch + c + 1], g[:, 8 * branch + 4 + c:8 * branch + 5 + c])

    def rank(self):
        tq, s0 = self.tq, self.s0
        n_blk = self.seq // SEL_BLOCK
        blk = lax.broadcasted_iota(jnp.int32, (n_blk, LANES), 0)
        pad = jnp.zeros((LANES - n_blk, LANES), F32)
        for g, part in [(g, part) for g in range(2) for part in range(tq // LANES)]:
            cur = (s0 + part * LANES + lax.broadcasted_iota(jnp.int32, (n_blk, LANES), 1)) // SEL_BLOCK
            valid = blk <= cur
            forced = (blk == 0) | (blk == cur) | (blk == cur - 1)
            imp = self.imps[g][:, part * LANES:(part + 1) * LANES]
            impv = jnp.where(valid, jnp.where(forced, BIG, imp), -1.0)
            n_slab = n_blk // 8
            slabs = [impv[8 * v:8 * (v + 1)] for v in range(n_slab)]
            cnts = [jnp.zeros((8, LANES), F32) for _ in range(n_slab)]
            for b in range(n_blk):
                other = impv[b:b + 1, :]
                for v in range(n_slab):
                    if b < 8 * v:
                        beats = other >= slabs[v]
                    elif b >= 8 * (v + 1):
                        beats = other > slabs[v]
                    else:
                        beats = (other > slabs[v]) | ((other == slabs[v]) & (blk[0:8] > b - 8 * v))
                    cnts[v] = cnts[v] + jnp.where(beats, 1.0, 0.0)
            cnt = jnp.concatenate(cnts, axis=0)
            bias_t = jnp.where(valid & (cnt < float(SEL_TOPK)), 0.0, NEG)
            bias = jnp.concatenate([bias_t, pad], axis=0).T.astype(BF16)
            for r in range(4 * g, 4 * g + 4):
                self.qaug[r * tq + part * LANES:r * tq + (part + 1) * LANES, LANES:2 * LANES] = bias

    def sel_init(self):
        self.m[...] = jnp.full((self.rows, LANES), NEG, F32)
        self.acc[...] = jnp.zeros((self.rows, LANES), F32)

    def sel_scores(self, kt, buf):
        k0 = pl.multiple_of(kt * TK_SEL, TK_SEL)
        k_aug = jnp.concatenate([self.kvsw_ref[pl.ds(k0, TK_SEL), 0:128], self.et_ref[pl.ds(k0, TK_SEL), :]], axis=1)
        buf[...] = _dot_nt(self.qaug[...], k_aug)

    def sel_tile(self, kt, buf, diagonal):
        tq, rows = self.tq, self.rows
        k0 = pl.multiple_of(kt * TK_SEL, TK_SEL)
        v = self.kvsw_ref[pl.ds(k0, TK_SEL), 128:256]
        s = buf[...]
        if diagonal:
            cb = self.cbias_ref[(self.s0 - k0) // tq]
            s = jnp.concatenate([s[r * tq:(r + 1) * tq] + cb for r in range(8)], axis=0)
        m_prev = self.m[...]
        m_new = jnp.maximum(m_prev, jnp.max(s, axis=-1, keepdims=True))
        alpha = jnp.exp2(m_prev - m_new)
        e = jnp.exp2((s - jnp.tile(m_new, (1, TK_SEL // LANES))).astype(BF16))
        v_lane = lax.broadcasted_iota(jnp.int32, v.shape, 1)
        one = jnp.ones(v.shape, BF16)
        pv = jnp.concatenate([_dot(e[:rows // 2], jnp.where(v_lane < HALF, v, one)),
                              _dot(e[rows // 2:], jnp.where(v_lane < HALF, one, v))], axis=0)
        self.acc[...] = alpha * self.acc[...] + pv
        self.m[...] = m_new

    def finish(self):
        tq = self.tq
        acc = self.acc[...]
        for c in range(4):
            o_sel = self.merged(acc, c) / self.merged_denominator(acc, c)
            self.o_ref[:, c * LANES:(c + 1) * LANES] = (self.part[c * tq:(c + 1) * tq] + self.gate(1, c) * o_sel).astype(BF16)


def _nsa_kernel(q_ref, kvsw_ref, kc_ref, vc_ref, gates_ref, et_ref, overlap_t_ref, wbias_ref, cbias_ref, o_ref,
                qaug_scr, m_scr, acc_scr, part_scr, sa_scr, sb_scr, *, seq):
    i = pl.program_id(1)
    a, b = [_NsaStream(st, i, seq, (q_ref, kvsw_ref, kc_ref, vc_ref, gates_ref, o_ref),
                       (et_ref, overlap_t_ref, wbias_ref, cbias_ref),
                       (qaug_scr, m_scr, acc_scr, part_scr, sa_scr, sb_scr)) for st in range(2)]
    a.prep()
    b.prep()
    a.cmp_scores()
    b.cmp_scores()
    a.win_scores()
    a.cmp_softmax()
    b.win_scores()
    b.cmp_softmax()
    a.rank()
    b.rank()
    a.sel_init()
    a.sel_scores(0, a.sa)
    a.win_softmax()
    b.sel_init()
    b.sel_scores(0, b.sa)
    b.win_softmax()

    last = (i * TQ_NSA) // TK_SEL

    def tile_pair(p, carry):
        kt = 2 * p
        for s in (a, b):
            s.sel_scores(kt + 1, s.sb)
        for s in (a, b):
            s.sel_tile(kt, s.sa, False)
        for s in (a, b):
            s.sel_scores(kt + 2, s.sa)
        for s in (a, b):
            s.sel_tile(kt + 1, s.sb, False)
        return carry

    lax.fori_loop(0, last // 2, tile_pair, 0)
    last_is_odd = (last & 1) == 1

    @pl.when(last_is_odd)
    def _():
        for s in (a, b):
            s.sel_scores(last, s.sb)
        for s in (a, b):
            s.sel_tile(last - 1, s.sa, False)
        for s in (a, b):
            s.sel_tile(last, s.sb, True)

    @pl.when(jnp.logical_not(last_is_odd))
    def _():
        for s in (a, b):
            s.sel_tile(last, s.sa, True)

    a.finish()
    b.finish()


def _mla_kernel(q_ref, k_ref, v_ref, dbias_ref, o_ref, m_scr, acc_scr):
    tq = TQ_MLA
    i = pl.program_id(1)
    s0 = i * tq
    m_scr[...] = jnp.full(m_scr.shape, NEG, F32)
    acc_scr[...] = jnp.zeros(acc_scr.shape, F32)

    def head_lanes(h):
        return slice(h * LANES, (h + 1) * LANES)

    def values(h, k0, nk):
        v2 = v_ref[pl.ds(k0, nk), head_lanes(h // 2)]
        lane = lax.broadcasted_iota(jnp.int32, (nk, LANES), 1)
        one = jnp.ones((nk, LANES), BF16)
        return jnp.where(lane < HALF, v2, one) if h % 2 == 0 else jnp.where(lane < HALF, one, v2)

    def update(h, r0, nr, s, vh):
        m_prev = m_scr[h, r0:r0 + nr]
        m_new = jnp.maximum(m_prev, jnp.max(s, axis=-1, keepdims=True))
        alpha = jnp.exp2(m_prev - m_new)
        e = jnp.exp2((s - jnp.tile(m_new, (1, s.shape[1] // LANES))).astype(BF16))
        acc_scr[h, r0:r0 + nr] = alpha * acc_scr[h, r0:r0 + nr] + _dot(e, vh)
        m_scr[h, r0:r0 + nr] = m_new

    def full_tile(kt, carry):
        k0 = pl.multiple_of(kt * TK_MLA, TK_MLA)
        scores = [_dot_nt(q_ref[:, head_lanes(h)], k_ref[pl.ds(k0, TK_MLA), head_lanes(h)])
                  for h in range(MLA_HEADS)]
        for h in range(MLA_HEADS):
            update(h, 0, tq, scores[h], values(h, k0, TK_MLA))
        return carry

    lax.fori_loop(0, s0 // TK_MLA, full_tile, 0)

    kd = pl.multiple_of(s0, tq)
    dbias = dbias_ref[...]
    scores = [_dot_nt(q_ref[:, head_lanes(h)], k_ref[pl.ds(kd, tq), head_lanes(h)]) for h in range(MLA_HEADS)]
    for h in range(MLA_HEADS):
        update(h, 0, tq, scores[h] + dbias, values(h, kd, tq))
    lo = lax.broadcasted_iota(jnp.int32, (tq, LANES), 1) < HALF
    for c in range(4):
        mixed = jnp.where(lo, acc_scr[2 * c], acc_scr[2 * c + 1])
        denom = jnp.where(lo, acc_scr[2 * c + 1], acc_scr[2 * c])
        o_ref[:, c * LANES:(c + 1) * LANES] = (mixed / pltpu.roll(denom, HALF, 1)).astype(BF16)


def _merge_kernel(x_ref, onsa_ref, omla_ref, wa_ref, wb_ref, wmg_ref, wout_ref, g_ref, b_ref, y_ref):
    n_sub = 4
    sub = x_ref.shape[0] // n_sub
    first = []
    for j in range(n_sub):
        rs = slice(j * sub, (j + 1) * sub)
        first.append((_dot(onsa_ref[rs, :], wa_ref[...]), _dot(omla_ref[rs, :], wb_ref[...]),
                      _dot(x_ref[rs, :].astype(BF16), wmg_ref[...])))
    for j in range(n_sub):
        rs = slice(j * sub, (j + 1) * sub)
        ya, yb, mg_logits = first[j]
        mg = _sigmoid(mg_logits)
        mixed = (mg[:, :D_MODEL] * ya + mg[:, D_MODEL:] * yb).astype(BF16)
        y_ref[rs, :] = _layer_norm(DEEPNORM_ALPHA * x_ref[rs, :] + _dot(mixed, wout_ref[...]), g_ref[...], b_ref[...])


def _ffn_kernel(x_ref, wg_ref, wu_ref, wd_ref, cw_ref, g_ref, b_ref, y_ref, tail_scr, h_scr):
    tm = TM_FFN
    nfc = D_FF // FC_FFN
    first_tile = pl.program_id(1) == 0
    x = x_ref[...]
    xb = x.astype(BF16)
    row = lax.broadcasted_iota(jnp.int32, (tm, FC_FFN), 0)

    @pl.when(first_tile)
    def _():
        tail_scr[...] = jnp.zeros(tail_scr.shape, F32)

    def gate_up(fc):
        cols = slice(fc * FC_FFN, (fc + 1) * FC_FFN)
        return _dot(xb, wg_ref[:, cols]), _dot(xb, wu_ref[:, cols])

    cur = gate_up(0)
    for fc in range(nfc):
        cols = slice(fc * FC_FFN, (fc + 1) * FC_FFN)
        nxt = gate_up(fc + 1) if fc + 1 < nfc else None
        a, up = cur
        tail = tail_scr[:, cols]
        tail_scr[:, cols] = a[tm - 8:, :]
        a1 = jnp.where(row == 0, tail[7:8, :], pltpu.roll(a, 1, 0))
        a2 = jnp.where(row == 0, tail[6:7, :], jnp.where(row == 1, tail[7:8, :], pltpu.roll(a, 2, 0)))
        cw = cw_ref[:, cols]
        conv = cw[0:1, :] * a2 + cw[1:2, :] * a1 + cw[2:3, :] * a + cw[3:4, :]
        h_scr[:, cols] = (_gelu_tanh(conv) * up).astype(BF16)
        cur = nxt
    half = tm // 2
    ys = [_dot(h_scr[j * half:(j + 1) * half, :], wd_ref[...]) for j in range(2)]
    for j in range(2):
        rs = slice(j * half, (j + 1) * half)
        y_ref[rs, :] = _layer_norm(DEEPNORM_ALPHA * x_ref[rs, :] + ys[j], g_ref[...], b_ref[...])


def _rope_tables(seq):
    pos = np.arange(seq, dtype=np.float64)

    def cs(rot):
        inv = ROPE_THETA ** (-np.arange(0, rot, 2, dtype=np.float64) / rot)
        ang = pos[:, None] * inv[None, :]
        return np.cos(ang), np.sin(ang)

    one = lambda n: np.ones((seq, n))
    zero = lambda n: np.zeros((seq, n))
    c8, s8 = cs(NSA_ROT_DIM)
    c16, s16 = cs(MLA_ROPE_DIM)
    nsa_c = np.concatenate([c8, c8, one(48)] * 2, axis=1)
    nsa_s = np.concatenate([-s8, s8, zero(48)] * 2, axis=1)
    mq_c = np.concatenate([one(64), c16, c16, one(32)], axis=1)
    mq_s = np.concatenate([zero(64), -s16, s16, zero(32)], axis=1)
    mk_c = np.concatenate([c16, c16, one(96)], axis=1)
    mk_s = np.concatenate([-s16, s16, zero(96)], axis=1)
    return jnp.asarray(np.stack([nsa_c, nsa_s, mq_c, mq_s, mk_c, mk_s]).astype(np.float32))


def _head_perm():
    return np.array([(c + 4 * half) * 64 + d for c in range(4) for half in range(2) for d in range(64)])


def _window_bias(tq):
    wk = WINDOW + tq
    out = np.zeros((WINDOW // tq + 1, tq, wk), np.float32)
    for v in range(WINDOW // tq + 1):
        s0 = v * tq
        w0 = max(s0 - WINDOW, 0)
        pos = s0 + np.arange(tq)[:, None]
        kpos = w0 + np.arange(wk)[None, :]
        out[v] = np.where((kpos <= pos) & (pos - kpos < WINDOW), 0.0, NEG)
    return jnp.asarray(out)


def _const(shape):
    nd = len(shape)
    return pl.BlockSpec(shape, lambda *_: (0,) * nd, pipeline_mode=pl.Buffered(1))


def _params(n_axes):
    return pltpu.CompilerParams(dimension_semantics=("arbitrary",) * n_axes, vmem_limit_bytes=VMEM_LIMIT)


def kernel(x, w_in, cmp_pe_k, cmp_pe_v, cmp_k_w1, cmp_k_b1, cmp_k_w2, cmp_v_w1, cmp_v_b1, cmp_v_w2,
           nsa_w_o, mla_q_norm, mla_w_uq, mla_kv_norm, mla_w_uk, mla_w_uv, mla_w_o, w_out,
           ln1_g, ln1_b, ffn_w_gate, ffn_w_up, ffn_conv_w, ffn_conv_b, ffn_w_down, ln2_g, ln2_b):
    b, s, d = x.shape
    assert d == D_MODEL and s % TK_SEL == 0 and s % TM_FFN == 0 and (s // CMP_STRIDE) % 8 == 0
    assert WINDOW % TQ_NSA == 0 and s >= WINDOW + TQ_NSA
    t = b * s
    perm = _head_perm()
    xf = x.reshape(t, d)
    for l in range(DEPTH):
        xf = _layer(xf, b, s, perm, w_in[l], cmp_pe_k[l], cmp_pe_v[l], cmp_k_w1[l], cmp_k_b1[l], cmp_k_w2[l],
                    cmp_v_w1[l], cmp_v_b1[l], cmp_v_w2[l], nsa_w_o[l], mla_q_norm[l], mla_w_uq[l],
                    mla_kv_norm[l], mla_w_uk[l], mla_w_uv[l], mla_w_o[l], w_out[l], ln1_g[l], ln1_b[l],
                    ffn_w_gate[l], ffn_w_up[l], ffn_conv_w[l], ffn_conv_b[l], ffn_w_down[l], ln2_g[l], ln2_b[l])
    return xf.reshape(b, s, d)


def _layer(xf, b, s, perm, w_in, pe_k, pe_v, ck_w1, ck_b1, ck_w2, cv_w1, cv_b1, cv_w2, nsa_w_o, q_norm, w_uq,
           kv_norm, w_uk, w_uv, mla_w_o, w_out, ln1_g, ln1_b, w_gate, w_up, conv_w, conv_b, w_down, ln2_g, ln2_b):
    t = b * s
    d = D_MODEL
    o = np.cumsum([0, 512, 128, 128, 128, 128, 128, 128, 24, MLA_Q_RANK, MLA_KV_RANK, MLA_ROPE_DIM, 2 * d])
    wq = w_in[:, o[0]:o[1]][:, perm].astype(BF16)
    wkv = w_in[:, o[1]:o[7]].astype(BF16)
    gate_cols = np.array([h * 3 + br for br in range(3) for h in range(8)])
    wg = jnp.pad(w_in[:, o[7]:o[8]][:, gate_cols], ((0, 0), (0, LANES - 24))).astype(BF16)
    wcq = w_in[:, o[8]:o[9]].astype(BF16)
    wckv = w_in[:, o[9]:o[10]].astype(BF16)
    wkr = jnp.pad(w_in[:, o[10]:o[11]], ((0, 0), (0, LANES - MLA_ROPE_DIM))).astype(BF16)
    wmg = w_in[:, o[11]:o[12]].astype(BF16)

    wuq = jnp.pad(w_uq.reshape(MLA_Q_RANK, MLA_HEADS, MLA_QK_DIM), ((0, 0), (0, 0), (0, LANES - MLA_QK_DIM)))
    wuq = wuq.reshape(MLA_Q_RANK, MLA_HEADS * LANES).astype(BF16)
    wuk = jnp.pad(w_uk.reshape(MLA_KV_RANK, MLA_HEADS, MLA_NOPE_DIM), ((0, 0), (0, 0), (0, LANES - MLA_NOPE_DIM)))
    wuk = wuk.reshape(MLA_KV_RANK, MLA_HEADS * LANES).astype(BF16)
    place = np.zeros((LANES, MLA_HEADS * LANES), np.float32)
    for h in range(MLA_HEADS):
        for r in range(MLA_ROPE_DIM):
            place[r, h * LANES + MLA_NOPE_DIM + r] = 1.0
    place = jnp.asarray(place, BF16)
    tabs = _rope_tables(s)

    tm = TM_PROJ
    n_s = s // tm
    row_blk = lambda w: pl.BlockSpec((tm, w), lambda i: (i, 0))
    outs = pl.pallas_call(
        _proj_kernel,
        grid=(t // tm,),
        in_specs=[row_blk(d), pl.BlockSpec((6, tm, LANES), lambda i: (0, i % n_s, 0)),
                  _const(wq.shape), _const(wkv.shape), _const(wg.shape), _const(wcq.shape), _const(wckv.shape),
                  _const(wkr.shape), _const((1, MLA_Q_RANK)), _const((1, MLA_KV_RANK)), _const(wuq.shape),
                  _const(wuk.shape), _const((MLA_KV_RANK, 512)), _const(place.shape)],
        out_specs=[row_blk(512), row_blk(128), row_blk(128), row_blk(512), row_blk(128),
                   row_blk(1024), row_blk(1024), row_blk(512)],
        out_shape=[jax.ShapeDtypeStruct((t, 512), BF16), jax.ShapeDtypeStruct((t, 128), F32),
                   jax.ShapeDtypeStruct((t, 128), F32), jax.ShapeDtypeStruct((t, 512), BF16),
                   jax.ShapeDtypeStruct((t, 128), F32), jax.ShapeDtypeStruct((t, 1024), BF16),
                   jax.ShapeDtypeStruct((t, 1024), BF16), jax.ShapeDtypeStruct((t, 512), BF16)],
        compiler_params=_params(1),
        name="proj",
    )(xf, tabs, wq, wkv, wg, wcq, wckv, wkr, q_norm.reshape(1, -1), kv_norm.reshape(1, -1), wuq, wuk,
      w_uv.astype(BF16), place)
    qnsa, kcmp, vcmp, kvsw, gates, qmla, kmla, vmla = outs

    nch = s // CMP_STRIDE
    flat = CMP_STRIDE * LANES

    def dup_w1(w1):
        w = w1.astype(BF16).reshape(2, CMP_STRIDE, 1, NSA_HEAD_DIM, CMP_HIDDEN)
        return jnp.broadcast_to(w, (2, CMP_STRIDE, 2, NSA_HEAD_DIM, CMP_HIDDEN)).reshape(2, flat, CMP_HIDDEN)

    def big_pe(pe):
        return jnp.tile(pe.reshape(2, CMP_STRIDE, 1, NSA_HEAD_DIM), (1, 1, 2, 1)).reshape(2, flat)

    def placed_w2(w2):
        w = w2.astype(BF16)
        z = jnp.zeros_like(w)
        return jnp.stack([jnp.concatenate([w, z], axis=1), jnp.concatenate([z, w], axis=1)])

    tok_blk = pl.BlockSpec((s, LANES), lambda i: (i, 0))
    cmp_blk = pl.BlockSpec((None, nch, LANES), lambda i: (i, 0, 0))
    kc, vc = pl.pallas_call(
        _compress_kernel,
        grid=(b,),
        in_specs=[tok_blk, tok_blk, _const((2, flat)), _const((2, flat)),
                  _const((2, flat, CMP_HIDDEN)), _const((2, flat, CMP_HIDDEN)),
                  _const((1, CMP_HIDDEN)), _const((1, CMP_HIDDEN)),
                  _const((2, CMP_HIDDEN, LANES)), _const((2, CMP_HIDDEN, LANES))],
        out_specs=[cmp_blk, cmp_blk],
        out_shape=[jax.ShapeDtypeStruct((b, nch, LANES), BF16)] * 2,
        compiler_params=_params(1),
        name="compress",
    )(kcmp, vcmp, big_pe(pe_k), big_pe(pe_v), dup_w1(ck_w1), dup_w1(cv_w1),
      ck_b1.reshape(1, -1), cv_b1.reshape(1, -1), placed_w2(ck_w2), placed_w2(cv_w2))

    n_sel = s // SEL_BLOCK
    key_blk = np.arange(s) // SEL_BLOCK
    onehot_t = jnp.asarray((key_blk[:, None] == np.arange(LANES)[None, :]).astype(np.float32), BF16)
    n_i = np.arange(nch)[None, :]
    j_i = np.arange(n_sel)[:, None]
    overlap_t = ((n_i <= 4 * j_i + 3) & (n_i >= 4 * j_i - 1) & (n_i < nch - 1)).astype(np.float32)
    overlap_t = jnp.asarray(overlap_t, BF16)
    tq = TQ_NSA
    nq = s // tq
    wbias = _window_bias(tq)
    r_i = np.arange(tq)[None, :, None]
    c_i = np.arange(TK_SEL)[None, None, :]
    v_i = np.arange(TK_SEL // tq)[:, None, None]
    cbias = jnp.asarray(np.where(c_i <= v_i * tq + r_i, 0.0, NEG).astype(np.float32))
    assert b % 2 == 0
    pair = lambda arr: arr.reshape((b // 2, 2) + arr.shape[1:])
    seq_blk = lambda rows_, w: pl.BlockSpec((None, 2, rows_, w), lambda bi, i: (bi, 0, i, 0))
    whole_blk = lambda rows_, w: pl.BlockSpec((None, 2, rows_, w), lambda bi, i: (bi, 0, 0, 0))
    onsa = pl.pallas_call(
        functools.partial(_nsa_kernel, seq=s),
        grid=(b // 2, nq),
        in_specs=[seq_blk(tq, 512), whole_blk(s, 512), whole_blk(nch, LANES), whole_blk(nch, LANES),
                  seq_blk(tq, LANES),
                  _const(onehot_t.shape), _const(overlap_t.shape), _const(wbias.shape), _const(cbias.shape)],
        out_specs=seq_blk(tq, 512),
        out_shape=jax.ShapeDtypeStruct((b // 2, 2, s, 512), BF16),
        scratch_shapes=[pltpu.VMEM((2, 8 * tq, 2 * LANES), BF16), pltpu.VMEM((2, 8 * tq, LANES), F32),
                        pltpu.VMEM((2, 8 * tq, LANES), F32),
                        pltpu.VMEM((2, 4 * tq, LANES), F32), pltpu.VMEM((2, 8 * tq, TK_SEL), F32),
                        pltpu.VMEM((2, 8 * tq, TK_SEL), F32)],
        compiler_params=_params(2),
        name="nsa",
    )(pair(qnsa.reshape(b, s, 512)), pair(kvsw.reshape(b, s, 512)), pair(kc), pair(vc),
      pair(gates.reshape(b, s, LANES)), onehot_t, overlap_t, wbias, cbias).reshape(t, 512)

    tq = TQ_MLA
    nq = s // tq
    assert tq == TK_MLA
    dbias = jnp.asarray(np.where(np.arange(tq)[None, :] <= np.arange(tq)[:, None], 0.0, NEG)
                        .astype(np.float32))
    omla = pl.pallas_call(
        _mla_kernel,
        grid=(b, nq),
        in_specs=[pl.BlockSpec((tq, 1024), lambda bi, i: (bi * nq + i, 0)),
                  pl.BlockSpec((s, 1024), lambda bi, i: (bi, 0)),
                  pl.BlockSpec((s, 512), lambda bi, i: (bi, 0)), _const(dbias.shape)],
        out_specs=pl.BlockSpec((tq, 512), lambda bi, i: (bi * nq + i, 0)),
        out_shape=jax.ShapeDtypeStruct((t, 512), BF16),
        scratch_shapes=[pltpu.VMEM((MLA_HEADS, tq, LANES), F32), pltpu.VMEM((MLA_HEADS, tq, LANES), F32)],
        compiler_params=_params(2),
        name="mla",
    )(qmla, kmla, vmla, dbias)

    tm = TM_MERGE
    row_blk = lambda w: pl.BlockSpec((tm, w), lambda i: (i, 0))
    x1 = pl.pallas_call(
        _merge_kernel,
        grid=(t // tm,),
        in_specs=[row_blk(d), row_blk(512), row_blk(512), _const((512, d)), _const((512, d)),
                  _const((d, 2 * d)), _const((d, d)), _const((1, d)), _const((1, d))],
        out_specs=row_blk(d),
        out_shape=jax.ShapeDtypeStruct((t, d), F32),
        compiler_params=_params(1),
        name="merge",
    )(xf, onsa, omla, nsa_w_o[perm, :].astype(BF16), mla_w_o.astype(BF16), wmg, w_out.astype(BF16),
      ln1_g.reshape(1, -1), ln1_b.reshape(1, -1))

    tm = TM_FFN
    nfc = D_FF // FC_FFN
    n_s = s // tm
    conv_tab = jnp.concatenate([conv_w, conv_b.reshape(1, -1), jnp.zeros((4, D_FF), F32)], axis=0)
    blk = pl.BlockSpec((tm, d), lambda bi, i: (bi * n_s + i, 0))
    y = pl.pallas_call(
        _ffn_kernel,
        grid=(b, n_s),
        in_specs=[blk, _const((d, D_FF)), _const((d, D_FF)), _const((D_FF, d)), _const((8, D_FF)),
                  _const((1, d)), _const((1, d))],
        out_specs=blk,
        out_shape=jax.ShapeDtypeStruct((t, d), F32),
        scratch_shapes=[pltpu.VMEM((8, D_FF), F32), pltpu.VMEM((tm, D_FF), BF16)],
        compiler_params=_params(2),
        name="ffn",
    )(x1, w_gate.astype(BF16), w_up.astype(BF16), w_down.astype(BF16), conv_tab,
      ln2_g.reshape(1, -1), ln2_b.reshape(1, -1))
    return y
```
